```python
import jax
import jax.numpy as jnp
from jax import lax
import numpy as np

D_MODEL = 2048
BATCH = 2
SEQ = 16384
DEPTH = 2

CHUNK = 64
EPS = 1e-6
MAX_POS_OFFSET = 4096
CONV_W = 1024
CONV_K = 3
SG_W = 1024
SG_BLOCK = 128
SG_GROUPS = 8
SG_GD = SG_W // SG_GROUPS
MLA_HEADS = 16
MLA_NOPE = 128
MLA_ROPE = 64
MLA_V = 128
MLA_KV_RANK = 512
ROPE_THETA = 10000.0
Q_BLOCK = 128
ML_HEADS = 4
ML_QK = 128
ML_V = 256
D_FF = 5632
N_EXPERTS = 8
TOP_K = 2
D_FF_EXPERT = 7168
MOE_BLOCK = 256
N_DENSE = (DEPTH + 1) // 2
N_MOE = DEPTH // 2
IN_SIZES = (
    CONV_W, CONV_W, CONV_W,
    SG_W, SG_W,
    MLA_HEADS * (MLA_NOPE + MLA_ROPE), MLA_KV_RANK, MLA_ROPE,
    ML_HEADS * ML_QK, ML_HEADS * ML_QK, ML_HEADS * ML_V, ML_HEADS * ML_V, ML_HEADS, ML_HEADS,
    D_MODEL, D_MODEL, D_MODEL, D_MODEL,
)
IN_W = sum(IN_SIZES)
IN_OFFSETS = tuple(sum(IN_SIZES[:j + 1]) for j in range(len(IN_SIZES) - 1))
ML_F_OFF = sum(IN_SIZES[:13])

kernel_name = 'hybrid_parallel_mixer_moe_block'


def rmsnorm(x, gain):
    xf = x.astype(jnp.float32)
    y = xf * lax.rsqrt(jnp.mean(xf * xf, axis=-1, keepdims=True) + EPS)
    return (y * gain.astype(jnp.float32)).astype(x.dtype)


def rope(x, positions):
    half = x.shape[-1] // 2
    freq = ROPE_THETA ** (-jnp.arange(half, dtype=jnp.float32) / half)
    ang = positions.astype(jnp.float32)[..., None] * freq
    ang = ang.reshape(ang.shape[:2] + (1,) * (x.ndim - 3) + (half,))
    cos, sin = jnp.cos(ang), jnp.sin(ang)
    xf = x.astype(jnp.float32)
    x1, x2 = xf[..., :half], xf[..., half:]
    return jnp.concatenate([x1 * cos - x2 * sin, x2 * cos + x1 * sin], axis=-1).astype(x.dtype)


def short_conv(gate_b, gate_c, h, conv_w):
    z = gate_c * h
    y = lax.conv_general_dilated(z, conv_w[:, None, :], window_strides=(1,), padding=[(CONV_K - 1, 0)],
                                 dimension_numbers=('NWC', 'WIO', 'NWC'), feature_group_count=CONV_W)
    return gate_b * y


def spatial_gate(u, v, norm_gain, w_s, b_s):
    B_, S_, _ = v.shape
    u = jax.nn.gelu(u)
    v = rmsnorm(jax.nn.gelu(v), norm_gain)
    vb = v.reshape(B_, S_ // SG_BLOCK, SG_BLOCK, SG_GROUPS, SG_GD)
    causal = jnp.tril(jnp.ones((SG_BLOCK, SG_BLOCK), bool))
    w = jnp.where(causal, w_s, jnp.zeros((), w_s.dtype))
    mixed = jnp.einsum('gts,bnsgd->bntgd', w, vb) + b_s.T[None, None, :, :, None]
    return u * mixed.reshape(B_, S_, SG_W)


def mla(q, ckv, k_rope, positions, kv_norm, w_uk, w_uv):
    B_, S_, _ = q.shape
    q = q.reshape(B_, S_, MLA_HEADS, MLA_NOPE + MLA_ROPE)
    q_nope = q[..., :MLA_NOPE]
    q_rope = rope(q[..., MLA_NOPE:], positions)
    k_rope = rope(k_rope, positions)
    ckv = rmsnorm(ckv, kv_norm)
    k_nope = jnp.einsum('bsc,chd->bshd', ckv, w_uk)
    v = jnp.einsum('bsc,chd->bshd', ckv, w_uv)
    nb = S_ // Q_BLOCK
    scale = (MLA_NOPE + MLA_ROPE) ** -0.5
    qn = q_nope.reshape(B_, nb, Q_BLOCK, MLA_HEADS, MLA_NOPE).swapaxes(0, 1)
    qr = q_rope.reshape(B_, nb, Q_BLOCK, MLA_HEADS, MLA_ROPE).swapaxes(0, 1)
    key_chunk = jnp.arange(S_) // CHUNK

    def attend(args):
        blk, qn_b, qr_b = args
        s = jnp.einsum('bqhd,bkhd->bhqk', qn_b, k_nope, preferred_element_type=jnp.float32)
        s = s + jnp.einsum('bqhr,bkr->bhqk', qr_b, k_rope, preferred_element_type=jnp.float32)
        q_chunk = (blk * Q_BLOCK + jnp.arange(Q_BLOCK)) // CHUNK
        s = jnp.where(key_chunk[None, :] <= q_chunk[:, None], s * scale, -jnp.inf)
        p = jax.nn.softmax(s, axis=-1).astype(v.dtype)
        return jnp.einsum('bhqk,bkhd->bqhd', p, v)

    o = lax.map(attend, (jnp.arange(nb), qn, qr))
    return o.swapaxes(0, 1).reshape(B_, S_, MLA_HEADS * MLA_V)


def mlstm(q, k, v, o, i_pre, f_pre, norm_gain):
    B_, S_, _ = q.shape
    nc = S_ // CHUNK
    f32 = jnp.float32

    def heads(t, d):
        return t.reshape(B_, nc, CHUNK, ML_HEADS, d).transpose(0, 3, 1, 2, 4).astype(f32)

    def gates(t):
        return t.reshape(B_, nc, CHUNK, ML_HEADS).transpose(0, 3, 1, 2).astype(f32)

    qh = heads(q, ML_QK)
    kh = heads(k, ML_QK) * (ML_QK ** -0.5)
    vh = heads(v, ML_V)
    ig = gates(i_pre)
    logf = jax.nn.log_sigmoid(gates(f_pre))
    b = jnp.cumsum(logf, axis=-1)
    b_end = b[..., -1]
    a = b_end[..., None] - b + ig
    g = jnp.max(a, axis=-1)
    wa = jnp.exp(a - g[..., None])
    kv_c = jnp.einsum('bhnl,bhnlv,bhnlk->bhnvk', wa, vh, kh)
    n_c = jnp.einsum('bhnl,bhnlk->bhnk', wa, kh)

    def step(carry, inp):
        C, n, m = carry
        kv_i, n_i, g_i, be_i = inp
        m_new = jnp.maximum(be_i + m, g_i)
        decay = jnp.exp(be_i + m - m_new)
        inject = jnp.exp(g_i - m_new)
        C_new = decay[..., None, None] * C + inject[..., None, None] * kv_i
        n_new = decay[..., None] * n + inject[..., None] * n_i
        return (C_new, n_new, m_new), (C, n, m)

    init = (jnp.zeros((B_, ML_HEADS, ML_V, ML_QK), f32), jnp.zeros((B_, ML_HEADS, ML_QK), f32),
            jnp.zeros((B_, ML_HEADS), f32))
    xs = tuple(jnp.moveaxis(t, 2, 0) for t in (kv_c, n_c, g, b_end))
    _, (C0, n0, m0) = lax.scan(step, init, xs)
    C0 = jnp.moveaxis(C0, 0, 2)
    n0 = jnp.moveaxis(n0, 0, 2)
    m0 = jnp.moveaxis(m0, 0, 2)
    causal = jnp.tril(jnp.ones((CHUNK, CHUNK), bool))
    d_log = jnp.where(causal, b[..., :, None] - b[..., None, :] + ig[..., None, :], -jnp.inf)
    inter_log = b + m0[..., None]
    m_t = jnp.maximum(inter_log, jnp.max(d_log, axis=-1))
    inter_w = jnp.exp(inter_log - m_t)
    qk = jnp.einsum('bhntk,bhnsk->bhnts', qh, kh) * jnp.exp(d_log - m_t[..., None])
    num = jnp.einsum('bhnts,bhnsv->bhntv', qk, vh) + inter_w[..., None] * jnp.einsum('bhnvk,bhntk->bhntv', C0, qh)
    den = jnp.sum(qk, axis=-1) + inter_w * jnp.einsum('bhnk,bhntk->bhnt', n0, qh)
    h = num / jnp.maximum(jnp.abs(den), jnp.exp(-m_t))[..., None]
    h = h.transpose(0, 2, 3, 1, 4).reshape(B_, S_, ML_HEADS, ML_V)
    h = rmsnorm(h, norm_gain.reshape(ML_HEADS, ML_V)).reshape(B_, S_, ML_HEADS * ML_V)
    return (jax.nn.sigmoid(o.astype(f32)) * h).astype(q.dtype)


def token_mixer(h, positions, w_in, b_in, conv_w, sg_norm, sg_w, sg_b, mla_kv_norm, mla_w_uk, mla_w_uv,
                ml_norm, w_conv_out, w_sg_out, w_mla_out, w_ml_out, w_mix_out):
    z = h @ w_in + b_in
    (cb, cc, ch, su, sv, mq, mckv, mkr, lq, lk, lv, lo, li, lf,
     g_conv, g_sg, g_mla, g_ml) = jnp.split(z, IN_OFFSETS, axis=-1)
    y_conv = short_conv(cb, cc, ch, conv_w)
    y_sg = spatial_gate(su, sv, sg_norm, sg_w, sg_b)
    y_mla = mla(mq, mckv, mkr, positions, mla_kv_norm, mla_w_uk, mla_w_uv)
    y_ml = mlstm(lq, lk, lv, lo, li, lf, ml_norm)
    merged = (jax.nn.sigmoid(g_conv) * (y_conv @ w_conv_out)
              + jax.nn.sigmoid(g_sg) * (y_sg @ w_sg_out)
              + jax.nn.sigmoid(g_mla) * (y_mla @ w_mla_out)
              + jax.nn.sigmoid(g_ml) * (y_ml @ w_ml_out))
    return merged @ w_mix_out


def swiglu(h, w_gate, w_up, w_down):
    return (jax.nn.silu(h @ w_gate) * (h @ w_up)) @ w_down


def moe(h, router_w, router_b, w_gate, w_up, w_down):
    B_, S_, D = h.shape
    T = B_ * S_
    xf = h.reshape(T, D)
    logits = (xf @ router_w).astype(jnp.float32) + router_b.astype(jnp.float32)
    top_v, top_e = lax.top_k(logits, TOP_K)
    top_w = jax.nn.softmax(top_v, axis=-1)
    n_asg = T * TOP_K
    e_flat = top_e.reshape(-1)
    tok_flat = jnp.arange(n_asg, dtype=jnp.int32) // TOP_K
    w_flat = top_w.reshape(-1)
    order = jnp.argsort(e_flat)
    e_sorted = e_flat[order]
    counts = jnp.zeros((N_EXPERTS,), jnp.int32).at[e_flat].add(1)
    padded = ((counts + MOE_BLOCK - 1) // MOE_BLOCK) * MOE_BLOCK
    start = jnp.cumsum(counts) - counts
    pend = jnp.cumsum(padded)
    pstart = pend - padded
    dest = pstart[e_sorted] + (jnp.arange(n_asg, dtype=jnp.int32) - start[e_sorted])
    n_blocks = n_asg // MOE_BLOCK + N_EXPERTS
    n_slots = n_blocks * MOE_BLOCK
    slot_tok = jnp.zeros((n_slots,), jnp.int32).at[dest].set(tok_flat[order])
    slot_w = jnp.zeros((n_slots,), jnp.float32).at[dest].set(w_flat[order])
    block_e = jnp.minimum(jnp.searchsorted(pend, jnp.arange(n_blocks) * MOE_BLOCK, side='right'), N_EXPERTS - 1)

    def run(args):
        tok, e = args
        xb = xf[tok]
        return (jax.nn.silu(xb @ w_gate[e]) * (xb @ w_up[e])) @ w_down[e]

    yb = lax.map(run, (slot_tok.reshape(n_blocks, MOE_BLOCK), block_e))
    out = jnp.zeros((T, D), jnp.float32).at[slot_tok].add(yb.reshape(n_slots, D).astype(jnp.float32) * slot_w[:, None])
    return out.astype(h.dtype).reshape(B_, S_, D)


def setup_inputs(seed: int = 0) -> dict:
    key = jax.random.key(seed)
    ks = iter(jax.random.split(key, 40))

    def nrm(shape, scale):
        return jax.random.normal(next(ks), shape, jnp.float32) * scale

    def gain(shape):
        return 1.0 + nrm(shape, 0.01)

    L = DEPTH
    D = D_MODEL
    x = nrm((BATCH, SEQ, D), 1.0)
    c = nrm((BATCH, D), 1.0)
    offs = jax.random.randint(next(ks), (BATCH, 1), 0, MAX_POS_OFFSET, dtype=jnp.int32)
    positions = offs + jnp.arange(SEQ, dtype=jnp.int32)[None, :]
    b_in = nrm((L, IN_W), 0.01).at[:, ML_F_OFF:ML_F_OFF + ML_HEADS].add(jnp.linspace(3.0, 6.0, ML_HEADS))
    return {
        'x': x, 'c': c, 'positions': positions,
        'ada_w': nrm((L, D, 6 * D), D ** -0.5), 'ada_b': nrm((L, 6 * D), 0.01),
        'norm_pre_mix': gain((L, D)), 'norm_post_mix': gain((L, D)),
        'norm_pre_ffn': gain((L, D)), 'norm_post_ffn': gain((L, D)),
        'w_in': nrm((L, D, IN_W), D ** -0.5), 'b_in': b_in,
        'conv_w': nrm((L, CONV_K, CONV_W), CONV_K ** -0.5),
        'sg_norm': gain((L, SG_W)),
        'sg_w': nrm((L, SG_GROUPS, SG_BLOCK, SG_BLOCK), SG_BLOCK ** -0.5),
        'sg_b': gain((L, SG_GROUPS, SG_BLOCK)),
        'mla_kv_norm': gain((L, MLA_KV_RANK)),
        'mla_w_uk': nrm((L, MLA_KV_RANK, MLA_HEADS, MLA_NOPE), MLA_KV_RANK ** -0.5),
        'mla_w_uv': nrm((L, MLA_KV_RANK, MLA_HEADS, MLA_V), MLA_KV_RANK ** -0.5),
        'ml_norm': gain((L, ML_HEADS * ML_V)),
        'w_conv_out': nrm((L, CONV_W, D), CONV_W ** -0.5),
        'w_sg_out': nrm((L, SG_W, D), SG_W ** -0.5),
        'w_mla_out': nrm((L, MLA_HEADS * MLA_V, D), (MLA_HEADS * MLA_V) ** -0.5),
        'w_ml_out': nrm((L, ML_HEADS * ML_V, D), (ML_HEADS * ML_V) ** -0.5),
        'w_mix_out': nrm((L, D, D), D ** -0.5),
        'ffn_w_gate': nrm((N_DENSE, D, D_FF), D ** -0.5),
        'ffn_w_up': nrm((N_DENSE, D, D_FF), D ** -0.5),
        'ffn_w_down': nrm((N_DENSE, D_FF, D), D_FF ** -0.5),
        'router_w': nrm((N_MOE, D, N_EXPERTS), D ** -0.5),
        'router_b': nrm((N_MOE, N_EXPERTS), 0.01),
        'exp_w_gate': nrm((N_MOE, N_EXPERTS, D, D_FF_EXPERT), D ** -0.5),
        'exp_w_up': nrm((N_MOE, N_EXPERTS, D, D_FF_EXPERT), D ** -0.5),
        'exp_w_down': nrm((N_MOE, N_EXPERTS, D_FF_EXPERT, D), D_FF_EXPERT ** -0.5),
    }


def reference(x, c, positions, ada_w, ada_b, norm_pre_mix, norm_post_mix, norm_pre_ffn, norm_post_ffn,
              w_in, b_in, conv_w, sg_norm, sg_w, sg_b, mla_kv_norm, mla_w_uk, mla_w_uv, ml_norm,
              w_conv_out, w_sg_out, w_mla_out, w_ml_out, w_mix_out,
              ffn_w_gate, ffn_w_up, ffn_w_down, router_w, router_b, exp_w_gate, exp_w_up, exp_w_down):
    cs = jax.nn.silu(c)
    for l in range(DEPTH):
        mod = (cs @ ada_w[l] + ada_b[l])[:, None, :]
        sh1, sc1, g1, sh2, sc2, g2 = jnp.split(mod, 6, axis=-1)
        h = rmsnorm(x, norm_pre_mix[l]) * (1 + sc1) + sh1
        y = token_mixer(h, positions, w_in[l], b_in[l], conv_w[l], sg_norm[l], sg_w[l], sg_b[l],
                        mla_kv_norm[l], mla_w_uk[l], mla_w_uv[l], ml_norm[l],
                        w_conv_out[l], w_sg_out[l], w_mla_out[l], w_ml_out[l], w_mix_out[l])
        x = x + g1 * rmsnorm(y, norm_post_mix[l])
        h = rmsnorm(x, norm_pre_ffn[l]) * (1 + sc2) + sh2
        if l % 2 == 0:
            y = swiglu(h, ffn_w_gate[l // 2], ffn_w_up[l // 2], ffn_w_down[l // 2])
        else:
            y = moe(h, router_w[l // 2], router_b[l // 2], exp_w_gate[l // 2], exp_w_up[l // 2], exp_w_down[l // 2])
        x = x + g2 * rmsnorm(y, norm_post_ffn[l])
    return x
```

```python
import functools

import jax
import jax.numpy as jnp
import numpy as np
from jax import lax
from jax.experimental import pallas as pl
from jax.experimental.pallas import tpu as pltpu

F32 = jnp.float32
BF16 = jnp.bfloat16

D_MODEL = 2048
DEPTH = 2
CHUNK = 64
EPS = 1e-6
CONV_W = 1024
CONV_K = 3
SG_W = 1024
SG_BLOCK = 128
SG_GROUPS = 8
SG_GD = SG_W // SG_GROUPS
MLA_HEADS = 16
MLA_NOPE = 128
MLA_ROPE = 64
MLA_V = 128
MLA_KV_RANK = 512
ROPE_THETA = 10000.0
ML_HEADS = 4
ML_QK = 128
ML_V = 256
D_FF = 5632
N_EXPERTS = 8
TOP_K = 2
D_FF_EXPERT = 7168
IN_SIZES = (
    CONV_W, CONV_W, CONV_W, SG_W, SG_W,
    MLA_HEADS * (MLA_NOPE + MLA_ROPE), MLA_KV_RANK, MLA_ROPE,
    ML_HEADS * ML_QK, ML_HEADS * ML_QK, ML_HEADS * ML_V, ML_HEADS * ML_V, ML_HEADS, ML_HEADS,
    D_MODEL, D_MODEL, D_MODEL, D_MODEL,
)
IN_NAMES = ("cb", "cc", "ch", "su", "sv", "mq", "mckv", "mkr", "lq", "lk", "lv", "lo", "li", "lf",
            "g_conv", "g_sg", "g_mla", "g_ml")
IN_START = {n: sum(IN_SIZES[:j]) for j, n in enumerate(IN_NAMES)}
IN_SIZE = dict(zip(IN_NAMES, IN_SIZES))

LANES = 128
SUBLANES = 8
VMEM_LIMIT = 56 * 1024 * 1024
NEG_BIG = -1e30

Z_ORDER = (("qn", 2048), ("g_conv", 2048), ("g_sg", 2048), ("g_mla", 2048), ("g_ml", 2048),
           ("cb", 1024), ("cc", 1024), ("ch", 1024), ("su", 1024), ("sv", 1024), ("qr", 1024),
           ("lv", 1024), ("lo", 1024), ("ckv", 512), ("lq", 512), ("lk", 512),
           ("kr", 128), ("gates", 128), ("pad", 256))
Z_OFF = {}
_o = 0
for _n, _w in Z_ORDER:
    Z_OFF[_n] = _o
    _o += _w
NZ = _o
Z_W = dict(Z_ORDER)

TILES = dict(
    in_tm=512, in_tn=2048,
    prep_tm=512,
    attn_tq=512,
    ml_tm=512, ml_chunk=128,
    m1_tm=256,
    m2_tm=512,
    ffn_tm=512, ffn_tf=512,
    rt_tm=512,
    disp_tm=256,
    exp_tm=512, exp_tf=512,
    ada_tn=1024,
)


def _cparams(sem, vmem=VMEM_LIMIT):
    return pltpu.CompilerParams(dimension_semantics=sem, vmem_limit_bytes=vmem)


def _sigmoid(x):
    return 1.0 / (1.0 + jnp.exp(-x))


def _gelu_tanh(x):
    return 0.5 * x * (1.0 + jnp.tanh(np.float32(np.sqrt(2.0 / np.pi)) * (x + 0.044715 * (x * x * x))))


def _rms(x, gain):
    return x * lax.rsqrt(jnp.mean(x * x, axis=-1, keepdims=True) + EPS) * gain


def _ada_kernel(c_ref, w_ref, b_ref, o_ref):
    c = c_ref[...]
    cs = (c * _sigmoid(c)).astype(BF16)
    o_ref[0] = jnp.dot(cs, w_ref[0].astype(BF16), preferred_element_type=F32) + b_ref[0]


def _ada_mod(c, ada_w, ada_b):
    L, D, N = ada_w.shape
    B = c.shape[0]
    tn = min(TILES["ada_tn"], N)
    c_pad = jnp.zeros((SUBLANES, D), F32).at[:B].set(c)
    out = pl.pallas_call(
        _ada_kernel,
        out_shape=jax.ShapeDtypeStruct((L, SUBLANES, N), F32),
        grid=(L, N // tn),
        in_specs=[pl.BlockSpec((SUBLANES, D), lambda l, j: (0, 0)),
                  pl.BlockSpec((1, D, tn), lambda l, j: (l, 0, j)),
                  pl.BlockSpec((1, 1, tn), lambda l, j: (l, 0, j))],
        out_specs=pl.BlockSpec((1, SUBLANES, tn), lambda l, j: (l, 0, j)),
        compiler_params=_cparams(("arbitrary", "arbitrary")),
        name="ada_mod",
    )(c_pad, ada_w, ada_b.reshape(L, 1, N))
    return out[:, :B]


def _in_proj_kernel(x_ref, g_ref, sc_ref, sh_ref, w_ref, b_ref, o_ref, og_ref, h_ref, *, gates_tile, gates_off):
    @pl.when(pl.program_id(1) == 0)
    def _():
        y = _rms(x_ref[...], g_ref[...])
        h_ref[...] = (y * (1.0 + sc_ref[0]) + sh_ref[0]).astype(BF16)

    acc = jnp.dot(h_ref[...], w_ref[...], preferred_element_type=F32) + b_ref[...]
    o_ref[...] = acc.astype(o_ref.dtype)

    @pl.when(pl.program_id(1) == gates_tile)
    def _():
        og_ref[...] = acc[:, gates_off:gates_off + LANES]


def _in_proj(x2, gain, sc, sh, w, b, S):
    T, D = x2.shape
    N = w.shape[1]
    tm = min(TILES["in_tm"], S)
    tn = min(TILES["in_tn"], N)
    spb = S // tm
    kern = functools.partial(_in_proj_kernel, gates_tile=Z_OFF["gates"] // tn, gates_off=Z_OFF["gates"] % tn)
    return pl.pallas_call(
        kern,
        out_shape=(jax.ShapeDtypeStruct((T, N), BF16), jax.ShapeDtypeStruct((T, LANES), F32)),
        grid=(T // tm, N // tn),
        in_specs=[pl.BlockSpec((tm, D), lambda i, j: (i, 0)),
                  pl.BlockSpec((1, D), lambda i, j: (0, 0)),
                  pl.BlockSpec((1, 1, D), lambda i, j: (i // spb, 0, 0)),
                  pl.BlockSpec((1, 1, D), lambda i, j: (i // spb, 0, 0)),
                  pl.BlockSpec((D, tn), lambda i, j: (0, j)),
                  pl.BlockSpec((1, tn), lambda i, j: (0, j))],
        out_specs=(pl.BlockSpec((tm, tn), lambda i, j: (i, j)),
                   pl.BlockSpec((tm, LANES), lambda i, j: (i, 0))),
        scratch_shapes=[pltpu.VMEM((tm, D), BF16)],
        compiler_params=_cparams(("arbitrary", "arbitrary")),
        name="in_proj",
    )(x2, gain.reshape(1, D), sc, sh, w, b.reshape(1, N))


def _swap_halves(v, first_half):
    return jnp.where(first_half, pltpu.roll(v, LANES - MLA_ROPE // 2, 1), pltpu.roll(v, MLA_ROPE // 2, 1))


def _mla_prep_kernel(qn_ref, qr_ref, ckv_ref, kr_ref, pos_ref, freq_ref, kvn_ref, wuk_ref, wuv_ref,
                     q_ref, k_ref, v_ref):
    tm = qn_ref.shape[0]
    ang = pos_ref[...].astype(F32) * freq_ref[...]
    cos = jnp.cos(ang)
    sin = jnp.sin(ang)
    lane = lax.broadcasted_iota(jnp.int32, (1, LANES), 1)
    first_half = (lane % MLA_ROPE) < (MLA_ROPE // 2)
    sgn_sin = jnp.where(first_half, -sin, sin)

    for c in range(MLA_HEADS // 2):
        v = qr_ref[:, c * LANES:(c + 1) * LANES].astype(F32)
        r = (v * cos + _swap_halves(v, first_half) * sgn_sin).astype(BF16)
        for h in (2 * c, 2 * c + 1):
            q_ref[0, h, :, 0:MLA_NOPE] = qn_ref[:, h * MLA_NOPE:(h + 1) * MLA_NOPE]
            q_ref[0, h, :, MLA_NOPE:2 * MLA_NOPE] = r

    cn = _rms(ckv_ref[...].astype(F32), kvn_ref[...]).astype(BF16)
    kn = jnp.dot(cn, wuk_ref[...], preferred_element_type=F32).astype(BF16)
    vv = jnp.dot(cn, wuv_ref[...], preferred_element_type=F32).astype(BF16)
    ka = kr_ref[...].astype(F32)
    kr_even = ka * cos + _swap_halves(ka, first_half) * sgn_sin
    kr_odd = pltpu.roll(kr_even, MLA_ROPE, 1)
    kr_even = kr_even.astype(BF16)
    kr_odd = kr_odd.astype(BF16)
    for h in range(MLA_HEADS):
        k_ref[0, h, :, 0:MLA_NOPE] = kn[:, h * MLA_NOPE:(h + 1) * MLA_NOPE]
        k_ref[0, h, :, MLA_NOPE:2 * MLA_NOPE] = kr_even if h % 2 == 0 else kr_odd
        v_ref[0, h] = vv[:, h * MLA_V:(h + 1) * MLA_V]
    del tm


def _mla_prep(z, pos, freq, kv_norm, w_uk, w_uv, B, S):
    tm = min(TILES["prep_tm"], S)
    spb = S // tm
    H = MLA_HEADS
    KD = 2 * MLA_NOPE

    def zspec(name):
        w = Z_W[name]
        cb = Z_OFF[name] // w
        return pl.BlockSpec((tm, w), lambda b, i: (b * spb + i, cb))

    return pl.pallas_call(
        _mla_prep_kernel,
        out_shape=(jax.ShapeDtypeStruct((B, H, S, KD), BF16),
                   jax.ShapeDtypeStruct((B, H, S, KD), BF16),
                   jax.ShapeDtypeStruct((B, H, S, MLA_V), BF16)),
        grid=(B, spb),
        in_specs=[zspec("qn"), zspec("qr"), zspec("ckv"), zspec("kr"),
                  pl.BlockSpec((tm, 1), lambda b, i: (b * spb + i, 0)),
                  pl.BlockSpec((1, LANES), lambda b, i: (0, 0)),
                  pl.BlockSpec((1, MLA_KV_RANK), lambda b, i: (0, 0)),
                  pl.BlockSpec((MLA_KV_RANK, H * MLA_NOPE), lambda b, i: (0, 0)),
                  pl.BlockSpec((MLA_KV_RANK, H * MLA_V), lambda b, i: (0, 0))],
        out_specs=(pl.BlockSpec((1, H, tm, KD), lambda b, i: (b, 0, i, 0)),
                   pl.BlockSpec((1, H, tm, KD), lambda b, i: (b, 0, i, 0)),
                   pl.BlockSpec((1, H, tm, MLA_V), lambda b, i: (b, 0, i, 0))),
        compiler_params=_cparams(("arbitrary", "arbitrary")),
        name="mla_prep",
    )(z, z, z, z, pos, freq, kv_norm.reshape(1, -1), w_uk, w_uv)


def _attn_kernel(q_ref, k_ref, v_ref, o_ref):
    tq = q_ref.shape[2]
    i = pl.program_id(2)
    q = q_ref[0, 0]

    def step(start, carry, masked):
        m, l, acc = carry
        k = k_ref[0, 0, pl.ds(start, tq), :]
        v = v_ref[0, 0, pl.ds(start, tq), :]
        s = lax.dot_general(q, k, (((1,), (1,)), ((), ())), preferred_element_type=F32)
        if masked:
            row = lax.broadcasted_iota(jnp.int32, (tq, tq), 0) // CHUNK
            col = lax.broadcasted_iota(jnp.int32, (tq, tq), 1) // CHUNK
            s = jnp.where(col <= row, s, NEG_BIG)
        m_new = jnp.maximum(m, jnp.max(s, axis=-1, keepdims=True))
        p = jnp.exp(s - m_new)
        alpha = jnp.exp(m - m_new)
        l = alpha * l + jnp.sum(p, axis=-1, keepdims=True)
        acc = alpha * acc + jnp.dot(p.astype(BF16), v, preferred_element_type=F32)
        return m_new, l, acc

    init = (jnp.full((tq, 1), NEG_BIG, F32), jnp.zeros((tq, 1), F32), jnp.zeros((tq, MLA_V), F32))
    carry = lax.fori_loop(0, i, lambda j, c: step(pl.multiple_of(j * tq, tq), c, False), init)
    m, l, acc = step(pl.multiple_of(i * tq, tq), carry, True)
    o_ref[0] = (acc / l).astype(o_ref.dtype)


def _attention(q, k, v):
    B, H, S, KD = q.shape
    tq = min(TILES["attn_tq"], S)
    return pl.pallas_call(
        _attn_kernel,
        out_shape=jax.ShapeDtypeStruct((B, S, H * MLA_V), BF16),
        grid=(B, H, S // tq),
        in_specs=[pl.BlockSpec((1, 1, tq, KD), lambda b, h, i: (b, h, i, 0)),
                  pl.BlockSpec((1, 1, S, KD), lambda b, h, i: (b, h, 0, 0)),
                  pl.BlockSpec((1, 1, S, MLA_V), lambda b, h, i: (b, h, 0, 0))],
        out_specs=pl.BlockSpec((1, tq, MLA_V), lambda b, h, i: (b, i, h)),
        compiler_params=_cparams(("arbitrary", "arbitrary", "arbitrary")),
        name="mla_attention",
    )(q, k, v)


ML_VA = ML_V + LANES


def _log_sigmoid(x):
    return -(jnp.maximum(-x, 0.0) + jnp.log1p(jnp.exp(-jnp.abs(x))))


def _mlstm_kernel(q_ref, k_ref, v_ref, o_ref, g_ref, gain_ref, y_ref, st_ref, m_ref):
    L = TILES["ml_chunk"]
    tm = q_ref.shape[0]
    assert L == LANES and tm % L == 0

    @pl.when(pl.program_id(1) == 0)
    def _():
        st_ref[...] = jnp.zeros_like(st_ref)
        m_ref[...] = jnp.zeros_like(m_ref)

    row = lax.broadcasted_iota(jnp.int32, (L, L), 0)
    col = lax.broadcasted_iota(jnp.int32, (L, L), 1)
    causal = col <= row
    tril = jnp.where(causal, 1.0, 0.0).astype(F32)
    triu = jnp.where(row <= col, 1.0, 0.0).astype(F32)
    lane_a = lax.broadcasted_iota(jnp.int32, (L, LANES), 1)
    ones_col = jnp.where(lane_a == 0, 1.0, 0.0).astype(BF16)

    def chunk(c, _):
        r0 = pl.multiple_of(c * L, L)
        gf = g_ref[pl.ds(r0, L), :]
        gt = gf.T
        lf_c = _log_sigmoid(gf)
        b_c = jnp.dot(tril, lf_c, precision=lax.Precision.HIGHEST, preferred_element_type=F32)
        lf_r = _log_sigmoid(gt)
        b_r = jnp.dot(lf_r, triu, precision=lax.Precision.HIGHEST, preferred_element_type=F32)
        for h in range(ML_HEADS):
            q = q_ref[pl.ds(r0, L), h * ML_QK:(h + 1) * ML_QK]
            k = k_ref[pl.ds(r0, L), h * ML_QK:(h + 1) * ML_QK]
            v = v_ref[pl.ds(r0, L), h * ML_V:(h + 1) * ML_V]
            vaug = jnp.concatenate([v, ones_col], axis=1)
            bcol = b_c[:, ML_HEADS + h:ML_HEADS + h + 1]
            icol = gf[:, h:h + 1]
            brow = b_r[ML_HEADS + h:ML_HEADS + h + 1, :]
            irow = gt[h:h + 1, :]
            m0 = m_ref[h][0:1, 0:1]
            dl = jnp.where(causal, bcol - brow + irow, NEG_BIG)
            inter_log = bcol + m0
            m_t = jnp.maximum(inter_log, jnp.max(dl, axis=-1, keepdims=True))
            dm = jnp.exp(dl - m_t)
            s = lax.dot_general(q, k, (((1,), (1,)), ((), ())), preferred_element_type=F32)
            p = (s * dm).astype(BF16)
            st = st_ref[h]
            intra = jnp.dot(p, vaug, preferred_element_type=F32)
            inter = jnp.dot(q, st.astype(BF16), preferred_element_type=F32)
            tot = intra + jnp.exp(inter_log - m_t) * inter
            num = tot[:, :ML_V]
            den = tot[:, ML_V:ML_V + 1]
            hv = num / jnp.maximum(jnp.abs(den), jnp.exp(-m_t))
            hn = _rms(hv, gain_ref[:, h * ML_V:(h + 1) * ML_V])
            og = o_ref[pl.ds(r0, L), h * ML_V:(h + 1) * ML_V].astype(F32)
            y_ref[pl.ds(r0, L), h * ML_V:(h + 1) * ML_V] = (_sigmoid(og) * hn).astype(y_ref.dtype)
            b_end = bcol[L - 1:L, :]
            a = b_end - bcol + icol
            g = jnp.max(a, axis=0, keepdims=True)
            wa = jnp.exp(a - g)
            kt = k.astype(F32).T.astype(BF16)
            upd = jnp.dot(kt, (vaug.astype(F32) * wa).astype(BF16), preferred_element_type=F32)
            m_new = jnp.maximum(b_end + m0, g)
            decay = jnp.exp(b_end + m0 - m_new)
            inject = jnp.exp(g - m_new)
            st_ref[h] = decay * st + inject * upd
            m_ref[h] = jnp.broadcast_to(m_new, (SUBLANES, LANES))
        return 0

    lax.fori_loop(0, tm // L, chunk, 0)


def _mlstm(z, zg, ml_norm, B, S):
    tm = min(TILES["ml_tm"], S)
    spb = S // tm
    T = B * S

    def zspec(name):
        w = Z_W[name]
        cb = Z_OFF[name] // w
        return pl.BlockSpec((tm, w), lambda b, i: (b * spb + i, cb))

    return pl.pallas_call(
        _mlstm_kernel,
        out_shape=jax.ShapeDtypeStruct((T, ML_HEADS * ML_V), BF16),
        grid=(B, spb),
        in_specs=[zspec("lq"), zspec("lk"), zspec("lv"), zspec("lo"),
                  pl.BlockSpec((tm, LANES), lambda b, i: (b * spb + i, 0)),
                  pl.BlockSpec((1, ML_HEADS * ML_V), lambda b, i: (0, 0))],
        out_specs=pl.BlockSpec((tm, ML_HEADS * ML_V), lambda b, i: (b * spb + i, 0)),
        scratch_shapes=[pltpu.VMEM((ML_HEADS, ML_QK, ML_VA), F32),
                        pltpu.VMEM((ML_HEADS, SUBLANES, LANES), F32)],
        compiler_params=_cparams(("arbitrary", "arbitrary")),
        name="mlstm",
    )(z, z, z, z, zg, ml_norm.reshape(1, -1))


def _merge_kernel(cb_ref, cc_ref, ch_ref, ccp_ref, chp_ref, su_ref, sv_ref,
                  gc_ref, gs_ref, gm_ref, gl_ref, ymla_ref, yml_ref,
                  convw_ref, sgn_ref, sgw_ref, sgb_ref, wc_ref, ws_ref, wm_ref, wl_ref,
                  o_ref, *, steps_per_batch):
    tm = cb_ref.shape[0]
    i = pl.program_id(0)
    z = cc_ref[...].astype(F32) * ch_ref[...].astype(F32)
    zp = ccp_ref[...].astype(F32) * chp_ref[...].astype(F32)
    zp = jnp.where(i % steps_per_batch == 0, 0.0, zp)
    rid = lax.broadcasted_iota(jnp.int32, (tm, 1), 0)
    z1 = jnp.where(rid == 0, zp[SUBLANES - 1:SUBLANES, :], pltpu.roll(z, 1, 0))
    z2 = jnp.where(rid == 0, zp[SUBLANES - 2:SUBLANES - 1, :],
                   jnp.where(rid == 1, zp[SUBLANES - 1:SUBLANES, :], pltpu.roll(z, 2, 0)))
    cw = convw_ref[...]
    y_conv = cb_ref[...].astype(F32) * (cw[0:1, :] * z2 + cw[1:2, :] * z1 + cw[2:3, :] * z)
    acc = _sigmoid(gc_ref[...].astype(F32)) * jnp.dot(y_conv.astype(BF16), wc_ref[...],
                                                      preferred_element_type=F32)
    u = _gelu_tanh(su_ref[...].astype(F32))
    vn = _rms(_gelu_tanh(sv_ref[...].astype(F32)), sgn_ref[...]).astype(BF16)
    r = lax.broadcasted_iota(jnp.int32, (SG_BLOCK, SG_BLOCK), 0)
    c = lax.broadcasted_iota(jnp.int32, (SG_BLOCK, SG_BLOCK), 1)
    blocks = []
    for n in range(tm // SG_BLOCK):
        cols = []
        for g in range(SG_GROUPS):
            w = jnp.where(c <= r, sgw_ref[g], 0.0).astype(BF16)
            vb = vn[n * SG_BLOCK:(n + 1) * SG_BLOCK, g * SG_GD:(g + 1) * SG_GD]
            cols.append(jnp.dot(w, vb, preferred_element_type=F32) + sgb_ref[:, g:g + 1])
        blocks.append(jnp.concatenate(cols, axis=1))
    mixed = blocks[0] if len(blocks) == 1 else jnp.concatenate(blocks, axis=0)
    y_sg = (u * mixed).astype(BF16)
    acc += _sigmoid(gs_ref[...].astype(F32)) * jnp.dot(y_sg, ws_ref[...], preferred_element_type=F32)
    acc += _sigmoid(gm_ref[...].astype(F32)) * jnp.dot(ymla_ref[...], wm_ref[...], preferred_element_type=F32)
    acc += _sigmoid(gl_ref[...].astype(F32)) * jnp.dot(yml_ref[...], wl_ref[...], preferred_element_type=F32)
    o_ref[...] = acc.astype(o_ref.dtype)


def _merge(z, y_mla, y_ml, conv_w, sg_norm, sg_w, sg_bt, wc, ws, wm, wl, S):
    T = z.shape[0]
    D = wc.shape[1]
    tm = min(TILES["m1_tm"], S)
    spb = S // tm
    rows8 = tm // SUBLANES

    def zspec(name):
        w = Z_W[name]
        cb = Z_OFF[name] // w
        return pl.BlockSpec((tm, w), lambda i: (i, cb))

    def zprev(name):
        w = Z_W[name]
        cb = Z_OFF[name] // w
        return pl.BlockSpec((SUBLANES, w), lambda i: (jnp.maximum(i * rows8 - 1, 0), cb))

    def const(shape):
        nd = len(shape)
        return pl.BlockSpec(shape, lambda i: (0,) * nd, pipeline_mode=pl.Buffered(1))

    return pl.pallas_call(
        functools.partial(_merge_kernel, steps_per_batch=spb),
        out_shape=jax.ShapeDtypeStruct((T, D), BF16),
        grid=(T // tm,),
        in_specs=[zspec("cb"), zspec("cc"), zspec("ch"), zprev("cc"), zprev("ch"), zspec("su"), zspec("sv"),
                  zspec("g_conv"), zspec("g_sg"), zspec("g_mla"), zspec("g_ml"),
                  pl.BlockSpec((tm, y_mla.shape[1]), lambda i: (i, 0)),
                  pl.BlockSpec((tm, y_ml.shape[1]), lambda i: (i, 0)),
                  const(conv_w.shape), const((1, SG_W)), const(sg_w.shape), const(sg_bt.shape),
                  const(wc.shape), const(ws.shape), const(wm.shape), const(wl.shape)],
        out_specs=pl.BlockSpec((tm, D), lambda i: (i, 0)),
        compiler_params=_cparams(("arbitrary",)),
        name="merge",
    )(z, z, z, z, z, z, z, z, z, z, z, y_mla, y_ml,
      conv_w, sg_norm.reshape(1, -1), sg_w, sg_bt, wc, ws, wm, wl)


def _mix_out_kernel(x_ref, m_ref, w_ref, gain_ref, gate_ref, o_ref):
    y = jnp.dot(m_ref[...], w_ref[...], preferred_element_type=F32)
    o_ref[...] = x_ref[...] + gate_ref[0] * _rms(y, gain_ref[...])


def _mix_out(x2, merged, w, gain, gate, S):
    T, D = x2.shape
    tm = min(TILES["m2_tm"], S)
    spb = S // tm
    return pl.pallas_call(
        _mix_out_kernel,
        out_shape=jax.ShapeDtypeStruct((T, D), F32),
        grid=(T // tm,),
        in_specs=[pl.BlockSpec((tm, D), lambda i: (i, 0)),
                  pl.BlockSpec((tm, D), lambda i: (i, 0)),
                  pl.BlockSpec((D, D), lambda i: (0, 0), pipeline_mode=pl.Buffered(1)),
                  pl.BlockSpec((1, D), lambda i: (0, 0)),
                  pl.BlockSpec((1, 1, D), lambda i: (i // spb, 0, 0))],
        out_specs=pl.BlockSpec((tm, D), lambda i: (i, 0)),
        compiler_params=_cparams(("arbitrary",)),
        name="mix_out",
    )(x2, merged, w, gain.reshape(1, D), gate)


def _ffn_kernel(x_ref, g_ref, sc_ref, sh_ref, wg_ref, wu_ref, wd_ref, gpost_ref, gate_ref, o_ref,
                h_ref, acc_ref):
    j = pl.program_id(1)

    @pl.when(j == 0)
    def _():
        y = _rms(x_ref[...], g_ref[...])
        h_ref[...] = (y * (1.0 + sc_ref[0]) + sh_ref[0]).astype(BF16)
        acc_ref[...] = jnp.zeros_like(acc_ref)

    h = h_ref[...]
    g = jnp.dot(h, wg_ref[...], preferred_element_type=F32)
    u = jnp.dot(h, wu_ref[...], preferred_element_type=F32)
    a = (g * _sigmoid(g) * u).astype(BF16)
    acc_ref[...] += jnp.dot(a, wd_ref[...], preferred_element_type=F32)

    @pl.when(j == pl.num_programs(1) - 1)
    def _():
        o_ref[...] = x_ref[...] + gate_ref[0] * _rms(acc_ref[...], gpost_ref[...])


def _ffn_dense(x2, gain_pre, sc, sh, wg, wu, wd, gain_post, gate, S):
    T, D = x2.shape
    F = wg.shape[1]
    tm = min(TILES["ffn_tm"], S)
    tf = min(TILES["ffn_tf"], F)
    spb = S // tm
    return pl.pallas_call(
        _ffn_kernel,
        out_shape=jax.ShapeDtypeStruct((T, D), F32),
        grid=(T // tm, F // tf),
        in_specs=[pl.BlockSpec((tm, D), lambda i, j: (i, 0)),
                  pl.BlockSpec((1, D), lambda i, j: (0, 0)),
                  pl.BlockSpec((1, 1, D), lambda i, j: (i // spb, 0, 0)),
                  pl.BlockSpec((1, 1, D), lambda i, j: (i // spb, 0, 0)),
                  pl.BlockSpec((D, tf), lambda i, j: (0, j)),
                  pl.BlockSpec((D, tf), lambda i, j: (0, j)),
                  pl.BlockSpec((tf, D), lambda i, j: (j, 0)),
                  pl.BlockSpec((1, D), lambda i, j: (0, 0)),
                  pl.BlockSpec((1, 1, D), lambda i, j: (i // spb, 0, 0))],
        out_specs=pl.BlockSpec((tm, D), lambda i, j: (i, 0)),
        scratch_shapes=[pltpu.VMEM((tm, D), BF16), pltpu.VMEM((tm, D), F32)],
        compiler_params=_cparams(("arbitrary", "arbitrary")),
        name="ffn_dense",
    )(x2, gain_pre.reshape(1, D), sc, sh, wg, wu, wd, gain_post.reshape(1, D), gate)


def _router_kernel(x_ref, g_ref, sc_ref, sh_ref, rw_ref, rb_ref, h_ref, meta_ref, wts_ref, cnt_ref, carry_ref):
    tm = x_ref.shape[0]

    @pl.when(pl.program_id(0) == 0)
    def _():
        carry_ref[...] = jnp.zeros_like(carry_ref)

    h = _rms(x_ref[...], g_ref[...]) * (1.0 + sc_ref[0]) + sh_ref[0]
    h_ref[...] = h
    logits = jnp.dot(h, rw_ref[...], precision=lax.Precision.HIGHEST, preferred_element_type=F32) + rb_ref[...]
    lane = lax.broadcasted_iota(jnp.int32, (tm, LANES), 1)
    m1 = jnp.max(logits, axis=-1, keepdims=True)
    i1 = jnp.min(jnp.where(logits == m1, lane, LANES), axis=-1, keepdims=True)
    oh1 = lane == i1
    rest = jnp.where(oh1, 2.0 * NEG_BIG, logits)
    m2 = jnp.max(rest, axis=-1, keepdims=True)
    i2 = jnp.min(jnp.where(rest == m2, lane, LANES), axis=-1, keepdims=True)
    oh2 = lane == i2
    e2 = jnp.exp(m2 - m1)
    den = 1.0 + e2
    w1 = 1.0 / den
    w2 = e2 / den
    oh = jnp.where(oh1 | oh2, 1.0, 0.0)
    r = lax.broadcasted_iota(jnp.int32, (tm, tm), 0)
    c = lax.broadcasted_iota(jnp.int32, (tm, tm), 1)
    strict = jnp.where(c < r, 1.0, 0.0).astype(BF16)
    carry = carry_ref[0:1, :]
    cum = jnp.dot(strict, oh.astype(BF16), preferred_element_type=F32) + carry
    rank1 = jnp.sum(jnp.where(oh1, cum, 0.0), axis=-1, keepdims=True).astype(jnp.int32)
    rank2 = jnp.sum(jnp.where(oh2, cum, 0.0), axis=-1, keepdims=True).astype(jnp.int32)
    new_carry = carry + jnp.sum(oh, axis=0, keepdims=True)
    carry_ref[...] = jnp.broadcast_to(new_carry, carry_ref.shape)
    cnt_ref[...] = jnp.broadcast_to(new_carry, cnt_ref.shape)
    meta_ref[...] = jnp.where(lane == 0, i1, jnp.where(lane == 1, i2, jnp.where(lane == 2, rank1,
                              jnp.where(lane == 3, rank2, 0))))
    wts_ref[...] = jnp.where(lane == 0, w1, jnp.where(lane == 1, w2, 0.0))


def _router(x2, gain, sc, sh, rw, rb, S):
    T, D = x2.shape
    tm = min(TILES["rt_tm"], S)
    spb = S // tm
    return pl.pallas_call(
        _router_kernel,
        out_shape=(jax.ShapeDtypeStruct((T, D), F32),
                   jax.ShapeDtypeStruct((T, LANES), jnp.int32),
                   jax.ShapeDtypeStruct((T, LANES), F32),
                   jax.ShapeDtypeStruct((SUBLANES, LANES), F32)),
        grid=(T // tm,),
        in_specs=[pl.BlockSpec((tm, D), lambda i: (i, 0)),
                  pl.BlockSpec((1, D), lambda i: (0, 0)),
                  pl.BlockSpec((1, 1, D), lambda i: (i // spb, 0, 0)),
                  pl.BlockSpec((1, 1, D), lambda i: (i // spb, 0, 0)),
                  pl.BlockSpec((D, LANES), lambda i: (0, 0)),
                  pl.BlockSpec((1, LANES), lambda i: (0, 0))],
        out_specs=(pl.BlockSpec((tm, D), lambda i: (i, 0)),
                   pl.BlockSpec((tm, LANES), lambda i: (i, 0)),
                   pl.BlockSpec((tm, LANES), lambda i: (i, 0)),
                   pl.BlockSpec((SUBLANES, LANES), lambda i: (0, 0))),
        scratch_shapes=[pltpu.VMEM((SUBLANES, LANES), F32)],
        compiler_params=_cparams(("arbitrary",)),
        name="moe_router",
    )(x2, gain.reshape(1, D), sc, sh, rw, rb)


def _row_copy(src_ref, src_row, dst_ref, dst_row, sem):
    return pltpu.make_async_copy(src_ref.at[pl.ds(src_row, 1)], dst_ref.at[pl.ds(dst_row, 1)], sem)


def _dispatch_kernel(dest_ref, h_ref, xs_in_ref, xs_ref, sem):
    del xs_in_ref
    tm = h_ref.shape[0]

    def issue(r, _):
        for kk in range(TOP_K):
            _row_copy(h_ref, r, xs_ref, dest_ref[0, 0, TOP_K * r + kk], sem).start()
        return 0

    lax.fori_loop(0, tm, issue, 0)

    def drain(r, _):
        for kk in range(TOP_K):
            _row_copy(h_ref, r, xs_ref, dest_ref[0, 0, TOP_K * r + kk], sem).wait()
        return 0

    lax.fori_loop(0, tm, drain, 0)


def _dispatch(h, dest, n_slots):
    T, D = h.shape
    tm = min(TILES["disp_tm"], T)
    nt = T // tm
    xs0 = jnp.zeros((n_slots, D), F32)
    return pl.pallas_call(
        _dispatch_kernel,
        out_shape=jax.ShapeDtypeStruct((n_slots, D), F32),
        grid=(nt,),
        in_specs=[pl.BlockSpec((1, 1, TOP_K * tm), lambda i: (i, 0, 0), memory_space=pltpu.SMEM),
                  pl.BlockSpec((tm, D), lambda i: (i, 0)),
                  pl.BlockSpec(memory_space=pl.ANY)],
        out_specs=pl.BlockSpec(memory_space=pl.ANY),
        scratch_shapes=[pltpu.SemaphoreType.DMA],
        input_output_aliases={2: 0},
        compiler_params=_cparams(("arbitrary",)),
        name="moe_dispatch",
    )(dest.reshape(nt, 1, TOP_K * tm), h, xs0)


def _expert_kernel(be_ref, nu_ref, xs_ref, wg_ref, wu_ref, wd_ref, o_ref, xb_ref, acc_ref):
    i = pl.program_id(0)
    j = pl.program_id(1)
    last = pl.num_programs(1) - 1
    active = i < nu_ref[0]

    @pl.when(active & (j == 0))
    def _():
        xb_ref[...] = xs_ref[...].astype(BF16)
        acc_ref[...] = jnp.zeros_like(acc_ref)

    @pl.when(active)
    def _():
        xb = xb_ref[...]
        g = jnp.dot(xb, wg_ref[0], preferred_element_type=F32)
        u = jnp.dot(xb, wu_ref[0], preferred_element_type=F32)
        a = (g * _sigmoid(g) * u).astype(BF16)
        acc_ref[...] += jnp.dot(a, wd_ref[0], preferred_element_type=F32)

    @pl.when(active & (j == last))
    def _():
        o_ref[...] = acc_ref[...]

    @pl.when(jnp.logical_not(active) & (j == last))
    def _():
        o_ref[...] = jnp.zeros_like(o_ref)


def _experts(xs, block_e, n_used, wg, wu, wd):
    n_slots, D = xs.shape
    E, _, F = wg.shape
    tm = TILES["exp_tm"]
    tf = min(TILES["exp_tf"], F)
    nb = n_slots // tm
    nj = F // tf

    def clamp(i, nu):
        return jnp.minimum(i, nu[0] - 1)

    def x_map(i, j, be, nu):
        return (clamp(i, nu), 0)

    def w_col_map(i, j, be, nu):
        return (be[clamp(i, nu)], 0, jnp.where(i < nu[0], j, nj - 1))

    def w_row_map(i, j, be, nu):
        return (be[clamp(i, nu)], jnp.where(i < nu[0], j, nj - 1), 0)

    return pl.pallas_call(
        _expert_kernel,
        out_shape=jax.ShapeDtypeStruct((n_slots, D), F32),
        grid_spec=pltpu.PrefetchScalarGridSpec(
            num_scalar_prefetch=2,
            grid=(nb, nj),
            in_specs=[pl.BlockSpec((tm, D), x_map),
                      pl.BlockSpec((1, D, tf), w_col_map),
                      pl.BlockSpec((1, D, tf), w_col_map),
                      pl.BlockSpec((1, tf, D), w_row_map)],
            out_specs=pl.BlockSpec((tm, D), lambda i, j, be, nu: (i, 0)),
            scratch_shapes=[pltpu.VMEM((tm, D), BF16), pltpu.VMEM((tm, D), F32)]),
        compiler_params=_cparams(("arbitrary", "arbitrary")),
        name="moe_experts",
    )(block_e, n_used, xs, wg, wu, wd)


def _combine_kernel(dest_ref, x_ref, wts_ref, gain_ref, gate_ref, ys_ref, o_ref, buf_ref, sem):
    tm = x_ref.shape[0]

    def issue(r, _):
        for kk in range(TOP_K):
            _row_copy(ys_ref, dest_ref[0, 0, TOP_K * r + kk], buf_ref.at[kk], r, sem).start()
        return 0

    lax.fori_loop(0, tm, issue, 0)

    def drain(r, _):
        for kk in range(TOP_K):
            _row_copy(ys_ref, dest_ref[0, 0, TOP_K * r + kk], buf_ref.at[kk], r, sem).wait()
        return 0

    lax.fori_loop(0, tm, drain, 0)
    w = wts_ref[...]
    y = w[:, 0:1] * buf_ref[0] + w[:, 1:2] * buf_ref[1]
    o_ref[...] = x_ref[...] + gate_ref[0] * _rms(y, gain_ref[...])


def _combine(x2, ys, dest, wts, gain, gate, S):
    T, D = x2.shape
    tm = min(TILES["disp_tm"], S)
    spb = S // tm
    nt = T // tm
    return pl.pallas_call(
        _combine_kernel,
        out_shape=jax.ShapeDtypeStruct((T, D), F32),
        grid=(nt,),
        in_specs=[pl.BlockSpec((1, 1, TOP_K * tm), lambda i: (i, 0, 0), memory_space=pltpu.SMEM),
                  pl.BlockSpec((tm, D), lambda i: (i, 0)),
                  pl.BlockSpec((tm, LANES), lambda i: (i, 0)),
                  pl.BlockSpec((1, D), lambda i: (0, 0)),
                  pl.BlockSpec((1, 1, D), lambda i: (i // spb, 0, 0)),
                  pl.BlockSpec(memory_space=pl.ANY)],
        out_specs=pl.BlockSpec((tm, D), lambda i: (i, 0)),
        scratch_shapes=[pltpu.VMEM((TOP_K, tm, D), F32), pltpu.SemaphoreType.DMA],
        compiler_params=_cparams(("arbitrary",)),
        name="moe_combine",
    )(dest.reshape(nt, 1, TOP_K * tm), x2, wts, gain.reshape(1, D), gate, ys)


def _moe(x2, gain_pre, sc, sh, router_w, router_b, wg, wu, wd, gain_post, gate, S):
    T, D = x2.shape
    E = router_w.shape[1]
    tm_e = TILES["exp_tm"]
    rw = jnp.zeros((D, LANES), F32).at[:, :E].set(router_w)
    rb = jnp.full((1, LANES), NEG_BIG, F32).at[0, :E].set(router_b)
    h, meta, wts, cnt = _router(x2, gain_pre, sc, sh, rw, rb, S)
    counts = cnt[0, :E].astype(jnp.int32)
    padded = ((counts + tm_e - 1) // tm_e) * tm_e
    pend = jnp.cumsum(padded)
    pstart = pend - padded
    dest = pstart[meta[:, 0:TOP_K]] + meta[:, TOP_K:2 * TOP_K]
    n_slots = T * TOP_K + E * tm_e
    nb = n_slots // tm_e
    block_e = jnp.minimum(jnp.searchsorted(pend, jnp.arange(nb, dtype=jnp.int32) * tm_e, side="right"),
                          E - 1).astype(jnp.int32)
    n_used = (pend[-1:] // tm_e).astype(jnp.int32)
    xs = _dispatch(h, dest.reshape(-1), n_slots)
    ys = _experts(xs, block_e, n_used, wg, wu, wd)
    return _combine(x2, ys, dest.reshape(-1), wts, gain_post, gate, S)


def _prep_in_proj(w_in, b_in):
    q_scale = np.float32((MLA_NOPE + MLA_ROPE) ** -0.5)
    k_scale = np.float32(ML_QK ** -0.5)

    def build(a):
        lead = a.shape[:-1]

        def seg(name):
            s = IN_START[name]
            return a[..., s:s + IN_SIZE[name]]

        mq = seg("mq").reshape(lead + (MLA_HEADS, MLA_NOPE + MLA_ROPE)) * q_scale
        parts = {
            "qn": mq[..., :MLA_NOPE].reshape(lead + (MLA_HEADS * MLA_NOPE,)),
            "qr": mq[..., MLA_NOPE:].reshape(lead + (MLA_HEADS * MLA_ROPE,)),
            "kr": jnp.concatenate([seg("mkr"), jnp.zeros(lead + (LANES - MLA_ROPE,), a.dtype)], axis=-1),
            "gates": jnp.concatenate([seg("li"), seg("lf"),
                                      jnp.zeros(lead + (LANES - 2 * ML_HEADS,), a.dtype)], axis=-1),
            "lk": seg("lk") * k_scale,
            "ckv": seg("mckv"),
            "pad": jnp.zeros(lead + (Z_W["pad"],), a.dtype),
        }
        cols = [parts[n] if n in parts else seg(n) for n, _ in Z_ORDER]
        return jnp.concatenate(cols, axis=-1)

    return build(w_in).astype(BF16), build(b_in)


def kernel(x, c, positions, ada_w, ada_b, norm_pre_mix, norm_post_mix, norm_pre_ffn, norm_post_ffn, w_in, b_in,
           conv_w, sg_norm, sg_w, sg_b, mla_kv_norm, mla_w_uk, mla_w_uv, ml_norm, w_conv_out, w_sg_out,
           w_mla_out, w_ml_out, w_mix_out, ffn_w_gate, ffn_w_up, ffn_w_down, router_w, router_b,
           exp_w_gate, exp_w_up, exp_w_down):
    B, S, D = x.shape
    T = B * S
    x2 = x.reshape(T, D)
    pos = positions.reshape(T, 1).astype(jnp.int32)
    half = MLA_ROPE // 2
    freq32 = ROPE_THETA ** (-jnp.arange(half, dtype=F32) / half)
    freq = jnp.tile(freq32, LANES // half).reshape(1, LANES)
    mod = _ada_mod(c, ada_w, ada_b)
    for l in range(DEPTH):
        sh1, sc1, g1, sh2, sc2, g2 = [m.reshape(B, 1, D) for m in jnp.split(mod[l], 6, axis=-1)]
        w_p, b_p = _prep_in_proj(w_in[l], b_in[l])
        z, zg = _in_proj(x2, norm_pre_mix[l], sc1, sh1, w_p, b_p, S)
        q, k, v = _mla_prep(z, pos, freq, mla_kv_norm[l],
                            mla_w_uk[l].reshape(MLA_KV_RANK, -1).astype(BF16),
                            mla_w_uv[l].reshape(MLA_KV_RANK, -1).astype(BF16), B, S)
        y_mla = _attention(q, k, v).reshape(T, MLA_HEADS * MLA_V)
        y_ml = _mlstm(z, zg, ml_norm[l], B, S)
        merged = _merge(z, y_mla, y_ml, conv_w[l], sg_norm[l], sg_w[l], sg_b[l].T,
                        w_conv_out[l].astype(BF16), w_sg_out[l].astype(BF16),
                        w_mla_out[l].astype(BF16), w_ml_out[l].astype(BF16), S)
        x2 = _mix_out(x2, merged, w_mix_out[l].astype(BF16), norm_post_mix[l], g1, S)
        if l % 2 == 0:
            x2 = _ffn_dense(x2, norm_pre_ffn[l], sc2, sh2, ffn_w_gate[l // 2].astype(BF16),
                            ffn_w_up[l // 2].astype(BF16), ffn_w_down[l // 2].astype(BF16),
                            norm_post_ffn[l], g2, S)
        else:
            x2 = _moe(x2, norm_pre_ffn[l], sc2, sh2, router_w[l // 2], router_b[l // 2],
                      exp_w_gate[l // 2].astype(BF16), exp_w_up[l // 2].astype(BF16),
                      exp_w_down[l // 2].astype(BF16), norm_post_ffn[l], g2, S)
    return x2.reshape(B, S, D)
```

```python
import functools

import jax
import jax.numpy as jnp
import numpy as np
from jax import lax
from jax.experimental import pallas as pl
from jax.experimental.pallas import tpu as pltpu

F32 = jnp.float32
BF16 = jnp.bfloat16

D_MODEL = 2048
DEPTH = 2
CHUNK = 64
EPS = 1e-6
CONV_W = 1024
CONV_K = 3
SG_W = 1024
SG_BLOCK = 128
SG_GROUPS = 8
SG_GD = SG_W // SG_GROUPS
MLA_HEADS = 16
MLA_NOPE = 128
MLA_ROPE = 64
MLA_V = 128
MLA_KV_RANK = 512
ROPE_THETA = 10000.0
ML_HEADS = 4
ML_QK = 128
ML_V = 256
D_FF = 5632
N_EXPERTS = 8
TOP_K = 2
D_FF_EXPERT = 7168
IN_SIZES = (
    CONV_W, CONV_W, CONV_W, SG_W, SG_W,
    MLA_HEADS * (MLA_NOPE + MLA_ROPE), MLA_KV_RANK, MLA_ROPE,
    ML_HEADS * ML_QK, ML_HEADS * ML_QK, ML_HEADS * ML_V, ML_HEADS * ML_V, ML_HEADS, ML_HEADS,
    D_MODEL, D_MODEL, D_MODEL, D_MODEL,
)
IN_NAMES = ("cb", "cc", "ch", "su", "sv", "mq", "mckv", "mkr", "lq", "lk", "lv", "lo", "li", "lf",
            "g_conv", "g_sg", "g_mla", "g_ml")
IN_START = {n: sum(IN_SIZES[:j]) for j, n in enumerate(IN_NAMES)}
IN_SIZE = dict(zip(IN_NAMES, IN_SIZES))

LANES = 128
SUBLANES = 8
VMEM_LIMIT = 56 * 1024 * 1024
NEG_BIG = -1e30

Z_ORDER = (("qn", 2048), ("g_conv", 2048), ("g_sg", 2048), ("g_mla", 2048), ("g_ml", 2048),
           ("cb", 1024), ("cc", 1024), ("ch", 1024), ("su", 1024), ("sv", 1024), ("qr", 1024),
           ("lv", 1024), ("lo", 1024), ("ckv", 512), ("lq", 512), ("lk", 512),
           ("kr", 128), ("gates", 128), ("pad", 256))
Z_OFF = {}
_o = 0
for _n, _w in Z_ORDER:
    Z_OFF[_n] = _o
    _o += _w
NZ = _o
Z_W = dict(Z_ORDER)

TILES = dict(
    in_tm=512, in_tn=2048,
    prep_tm=512,
    attn_tq=1024,
    ml_tm=512, ml_chunk=128,
    m1_tm=256,
    m2_tm=512,
    ffn_tm=512, ffn_tf=512,
    rt_tm=512,
    disp_tm=256,
    exp_tm=512, exp_tf=512,
    ada_tn=1024,
)


def _cparams(sem, vmem=VMEM_LIMIT):
    return pltpu.CompilerParams(dimension_semantics=sem, vmem_limit_bytes=vmem)


def _sigmoid(x):
    return 1.0 / (1.0 + jnp.exp(-x))


def _gelu_tanh(x):
    return 0.5 * x * (1.0 + jnp.tanh(np.float32(np.sqrt(2.0 / np.pi)) * (x + 0.044715 * (x * x * x))))


def _rms(x, gain):
    return x * lax.rsqrt(jnp.mean(x * x, axis=-1, keepdims=True) + EPS) * gain


def _ada_kernel(c_ref, w_ref, b_ref, o_ref):
    c = c_ref[...]
    cs = (c * _sigmoid(c)).astype(BF16)
    o_ref[0] = jnp.dot(cs, w_ref[0].astype(BF16), preferred_element_type=F32) + b_ref[0]


def _ada_mod(c, ada_w, ada_b):
    L, D, N = ada_w.shape
    B = c.shape[0]
    tn = min(TILES["ada_tn"], N)
    c_pad = jnp.zeros((SUBLANES, D), F32).at[:B].set(c)
    out = pl.pallas_call(
        _ada_kernel,
        out_shape=jax.ShapeDtypeStruct((L, SUBLANES, N), F32),
        grid=(L, N // tn),
        in_specs=[pl.BlockSpec((SUBLANES, D), lambda l, j: (0, 0)),
                  pl.BlockSpec((1, D, tn), lambda l, j: (l, 0, j)),
                  pl.BlockSpec((1, 1, tn), lambda l, j: (l, 0, j))],
        out_specs=pl.BlockSpec((1, SUBLANES, tn), lambda l, j: (l, 0, j)),
        compiler_params=_cparams(("arbitrary", "arbitrary")),
        name="ada_mod",
    )(c_pad, ada_w, ada_b.reshape(L, 1, N))
    return out[:, :B]


def _in_proj_kernel(x_ref, g_ref, sc_ref, sh_ref, w_ref, b_ref, o_ref, og_ref, h_ref, *, gates_tile, gates_off):
    @pl.when(pl.program_id(1) == 0)
    def _():
        y = _rms(x_ref[...], g_ref[...])
        h_ref[...] = (y * (1.0 + sc_ref[0]) + sh_ref[0]).astype(BF16)

    acc = jnp.dot(h_ref[...], w_ref[...], preferred_element_type=F32) + b_ref[...]
    o_ref[...] = acc.astype(o_ref.dtype)

    @pl.when(pl.program_id(1) == gates_tile)
    def _():
        og_ref[...] = acc[:, gates_off:gates_off + LANES]


def _in_proj(x2, gain, sc, sh, w, b, S):
    T, D = x2.shape
    N = w.shape[1]
    tm = min(TILES["in_tm"], S)
    tn = min(TILES["in_tn"], N)
    spb = S // tm
    kern = functools.partial(_in_proj_kernel, gates_tile=Z_OFF["gates"] // tn, gates_off=Z_OFF["gates"] % tn)
    return pl.pallas_call(
        kern,
        out_shape=(jax.ShapeDtypeStruct((T, N), BF16), jax.ShapeDtypeStruct((T, LANES), F32)),
        grid=(T // tm, N // tn),
        in_specs=[pl.BlockSpec((tm, D), lambda i, j: (i, 0)),
                  pl.BlockSpec((1, D), lambda i, j: (0, 0)),
                  pl.BlockSpec((1, 1, D), lambda i, j: (i // spb, 0, 0)),
                  pl.BlockSpec((1, 1, D), lambda i, j: (i // spb, 0, 0)),
                  pl.BlockSpec((D, tn), lambda i, j: (0, j)),
                  pl.BlockSpec((1, tn), lambda i, j: (0, j))],
        out_specs=(pl.BlockSpec((tm, tn), lambda i, j: (i, j)),
                   pl.BlockSpec((tm, LANES), lambda i, j: (i, 0))),
        scratch_shapes=[pltpu.VMEM((tm, D), BF16)],
        compiler_params=_cparams(("arbitrary", "arbitrary")),
        name="in_proj",
    )(x2, gain.reshape(1, D), sc, sh, w, b.reshape(1, N))


def _swap_halves(v, first_half):
    return jnp.where(first_half, pltpu.roll(v, LANES - MLA_ROPE // 2, 1), pltpu.roll(v, MLA_ROPE // 2, 1))


def _mla_prep_kernel(qn_ref, qr_ref, ckv_ref, kr_ref, pos_ref, freq_ref, kvn_ref, wuk_ref, wuv_ref,
                     q_ref, k_ref, v_ref):
    tm = qn_ref.shape[0]
    ang = pos_ref[...].astype(F32) * freq_ref[...]
    cos = jnp.cos(ang)
    sin = jnp.sin(ang)
    lane = lax.broadcasted_iota(jnp.int32, (1, LANES), 1)
    first_half = (lane % MLA_ROPE) < (MLA_ROPE // 2)
    sgn_sin = jnp.where(first_half, -sin, sin)

    for c in range(MLA_HEADS // 2):
        v = qr_ref[:, c * LANES:(c + 1) * LANES].astype(F32)
        r = (v * cos + _swap_halves(v, first_half) * sgn_sin).astype(BF16)
        for h in (2 * c, 2 * c + 1):
            q_ref[0, h, :, 0:MLA_NOPE] = qn_ref[:, h * MLA_NOPE:(h + 1) * MLA_NOPE]
            q_ref[0, h, :, MLA_NOPE:2 * MLA_NOPE] = r

    cn = _rms(ckv_ref[...].astype(F32), kvn_ref[...]).astype(BF16)
    knt = lax.dot_general(wuk_ref[...], cn, (((1,), (1,)), ((), ())), preferred_element_type=F32).astype(BF16)
    vv = jnp.dot(cn, wuv_ref[...], preferred_element_type=F32).astype(BF16)
    ka = kr_ref[...].astype(F32)
    kr_even = ka * cos + _swap_halves(ka, first_half) * sgn_sin
    kr_odd = pltpu.roll(kr_even, MLA_ROPE, 1)
    krt_even = kr_even.T.astype(BF16)
    krt_odd = kr_odd.T.astype(BF16)
    ones_col = jnp.where(lax.broadcasted_iota(jnp.int32, (tm, LANES), 1) == 0, 1.0, 0.0).astype(BF16)
    for h in range(MLA_HEADS):
        k_ref[0, h, 0:MLA_NOPE, :] = knt[h * MLA_NOPE:(h + 1) * MLA_NOPE, :]
        k_ref[0, h, MLA_NOPE:2 * MLA_NOPE, :] = krt_even if h % 2 == 0 else krt_odd
        v_ref[0, h, :, 0:MLA_V] = vv[:, h * MLA_V:(h + 1) * MLA_V]
        v_ref[0, h, :, MLA_V:MLA_V + LANES] = ones_col


def _mla_prep(z, pos, freq, kv_norm, w_uk_t, w_uv, B, S):
    tm = min(TILES["prep_tm"], S)
    spb = S // tm
    H = MLA_HEADS
    KD = 2 * MLA_NOPE

    def zspec(name):
        w = Z_W[name]
        cb = Z_OFF[name] // w
        return pl.BlockSpec((tm, w), lambda b, i: (b * spb + i, cb))

    return pl.pallas_call(
        _mla_prep_kernel,
        out_shape=(jax.ShapeDtypeStruct((B, H, S, KD), BF16),
                   jax.ShapeDtypeStruct((B, H, KD, S), BF16),
                   jax.ShapeDtypeStruct((B, H, S, MLA_V + LANES), BF16)),
        grid=(B, spb),
        in_specs=[zspec("qn"), zspec("qr"), zspec("ckv"), zspec("kr"),
                  pl.BlockSpec((tm, 1), lambda b, i: (b * spb + i, 0)),
                  pl.BlockSpec((1, LANES), lambda b, i: (0, 0)),
                  pl.BlockSpec((1, MLA_KV_RANK), lambda b, i: (0, 0)),
                  pl.BlockSpec((H * MLA_NOPE, MLA_KV_RANK), lambda b, i: (0, 0)),
                  pl.BlockSpec((MLA_KV_RANK, H * MLA_V), lambda b, i: (0, 0))],
        out_specs=(pl.BlockSpec((1, H, tm, KD), lambda b, i: (b, 0, i, 0)),
                   pl.BlockSpec((1, H, KD, tm), lambda b, i: (b, 0, 0, i)),
                   pl.BlockSpec((1, H, tm, MLA_V + LANES), lambda b, i: (b, 0, i, 0))),
        compiler_params=_cparams(("arbitrary", "arbitrary")),
        name="mla_prep",
    )(z, z, z, z, pos, freq, kv_norm.reshape(1, -1), w_uk_t, w_uv)


ATTN_GROUPS = 2
ATTN_STRIP = 32
ATTN_KTILE = 256


def _attn_kernel(q_ref, kt_ref, v_ref, o_ref, s0_ref, s1_ref, *scratch):
    G = ATTN_GROUPS
    p_refs, acc_refs, m_refs, bm_refs, al_refs = (scratch[n * G:(n + 1) * G] for n in range(5))
    tq = q_ref.shape[2]
    tk = tq // G
    va = v_ref.shape[3]
    i = pl.program_id(2)
    for g in range(G):
        m_refs[g][...] = jnp.full(m_refs[g].shape, NEG_BIG, F32)
        acc_refs[g][...] = jnp.zeros(acc_refs[g].shape, F32)

    def scores(blk, s_ref, first_row=0):
        start = pl.multiple_of(blk * tk, tk)
        s_ref[first_row:, :] = jnp.dot(q_ref[0, 0, first_row:, :], kt_ref[0, 0, :, pl.ds(start, tk)],
                                       preferred_element_type=F32)

    def softmax_pv(blk, s_ref, modes):
        v = v_ref[0, 0, pl.ds(pl.multiple_of(blk * tk, tk), tk), :]
        for g in range(G):
            if modes[g] == "skip":
                continue
            p_ref, acc_ref, m_ref, bm_ref, al_ref = p_refs[g], acc_refs[g], m_refs[g], bm_refs[g], al_refs[g]

            def load_strip(r, cols=slice(0, tk)):
                x = s_ref[g * tk + r * ATTN_STRIP:g * tk + (r + 1) * ATTN_STRIP, cols]
                if modes[g] == "diag":
                    row = (lax.broadcasted_iota(jnp.int32, x.shape, 0) + r * ATTN_STRIP) // CHUNK
                    col = (lax.broadcasted_iota(jnp.int32, x.shape, 1) + cols.start) // CHUNK
                    x = jnp.where(col <= row, x, NEG_BIG)
                return x

            for r in range(tk // ATTN_STRIP):
                rows = slice(r * ATTN_STRIP, (r + 1) * ATTN_STRIP)
                bm_ref[rows, :] = jnp.broadcast_to(jnp.max(load_strip(r), axis=-1, keepdims=True),
                                                   (ATTN_STRIP, LANES))
            m_old = m_ref[...]
            m_new = jnp.maximum(m_old, bm_ref[...])
            m_ref[...] = m_new
            al_ref[...] = jnp.exp2(m_old - m_new)
            pv = acc_ref[...] * jnp.concatenate([al_ref[...]] * (va // LANES), axis=1)
            kt = min(ATTN_KTILE, tk)
            for c in range(tk // kt):
                cols = slice(c * kt, (c + 1) * kt)
                for r in range(tk // ATTN_STRIP):
                    rows = slice(r * ATTN_STRIP, (r + 1) * ATTN_STRIP)
                    p = jnp.exp2(load_strip(r, cols) - jnp.concatenate([m_ref[rows, :]] * (kt // LANES), axis=1))
                    p_ref[rows, cols] = p.astype(BF16)
                pv = pv + jnp.dot(p_ref[:, cols], v[cols, :], preferred_element_type=F32)
            acc_ref[...] = pv

    scores(0, s0_ref)

    def body(t, carry):
        scores(2 * t + 1, s1_ref)
        softmax_pv(2 * t, s0_ref, ("full", "full"))
        scores(2 * t + 2, s0_ref)
        softmax_pv(2 * t + 1, s1_ref, ("full", "full"))
        return carry

    lax.fori_loop(0, i, body, 0)
    scores(2 * i + 1, s1_ref, first_row=tk)
    softmax_pv(2 * i, s0_ref, ("diag", "full"))
    softmax_pv(2 * i + 1, s1_ref, ("skip", "diag"))

    for g in range(G):
        acc = acc_refs[g][...]
        o_ref[0, g * tk:(g + 1) * tk, :] = (acc[:, :MLA_V] / acc[:, MLA_V:MLA_V + 1]).astype(o_ref.dtype)


def _attention(q, kt, v):
    B, H, S, KD = q.shape
    VA = v.shape[3]
    tq = min(TILES["attn_tq"], S)
    hq = tq // ATTN_GROUPS
    assert hq % CHUNK == 0 and hq % ATTN_STRIP == 0 and S % tq == 0
    return pl.pallas_call(
        _attn_kernel,
        out_shape=jax.ShapeDtypeStruct((B, S, H * MLA_V), BF16),
        grid=(B, H, S // tq),
        in_specs=[pl.BlockSpec((1, 1, tq, KD), lambda b, h, i: (b, h, i, 0)),
                  pl.BlockSpec((1, 1, KD, S), lambda b, h, i: (b, h, 0, 0)),
                  pl.BlockSpec((1, 1, S, VA), lambda b, h, i: (b, h, 0, 0))],
        out_specs=pl.BlockSpec((1, tq, MLA_V), lambda b, h, i: (b, i, h)),
        scratch_shapes=([pltpu.VMEM((tq, hq), F32)] * 2 + [pltpu.VMEM((hq, hq), BF16)] * ATTN_GROUPS
                        + [pltpu.VMEM((hq, VA), F32)] * ATTN_GROUPS
                        + [pltpu.VMEM((hq, LANES), F32)] * (3 * ATTN_GROUPS)),
        compiler_params=_cparams(("arbitrary", "arbitrary", "arbitrary")),
        name="mla_attention",
    )(q, kt, v)


ML_VA = ML_V + LANES


def _log_sigmoid(x):
    return -(jnp.maximum(-x, 0.0) + jnp.log1p(jnp.exp(-jnp.abs(x))))


def _mlstm_kernel(q_ref, k_ref, v_ref, o_ref, g_ref, gain_ref, y_ref, st_ref, m_ref):
    L = TILES["ml_chunk"]
    tm = q_ref.shape[0]
    assert L == LANES and tm % L == 0

    @pl.when(pl.program_id(1) == 0)
    def _():
        st_ref[...] = jnp.zeros_like(st_ref)
        m_ref[...] = jnp.zeros_like(m_ref)

    row = lax.broadcasted_iota(jnp.int32, (L, L), 0)
    col = lax.broadcasted_iota(jnp.int32, (L, L), 1)
    causal = col <= row
    tril = jnp.where(causal, 1.0, 0.0).astype(F32)
    triu = jnp.where(row <= col, 1.0, 0.0).astype(F32)
    lane_a = lax.broadcasted_iota(jnp.int32, (L, LANES), 1)
    ones_col = jnp.where(lane_a == 0, 1.0, 0.0).astype(BF16)

    def chunk(c, _):
        r0 = pl.multiple_of(c * L, L)
        gf = g_ref[pl.ds(r0, L), :]
        gt = gf.T
        lf_c = _log_sigmoid(gf)
        b_c = jnp.dot(tril, lf_c, precision=lax.Precision.HIGHEST, preferred_element_type=F32)
        lf_r = _log_sigmoid(gt)
        b_r = jnp.dot(lf_r, triu, precision=lax.Precision.HIGHEST, preferred_element_type=F32)
        for h in range(ML_HEADS):
            q = q_ref[pl.ds(r0, L), h * ML_QK:(h + 1) * ML_QK]
            k = k_ref[pl.ds(r0, L), h * ML_QK:(h + 1) * ML_QK]
            v = v_ref[pl.ds(r0, L), h * ML_V:(h + 1) * ML_V]
            vaug = jnp.concatenate([v, ones_col], axis=1)
            bcol = b_c[:, ML_HEADS + h:ML_HEADS + h + 1]
            icol = gf[:, h:h + 1]
            brow = b_r[ML_HEADS + h:ML_HEADS + h + 1, :]
            irow = gt[h:h + 1, :]
            m0 = m_ref[h][0:1, 0:1]
            dl = jnp.where(causal, bcol - brow + irow, NEG_BIG)
            inter_log = bcol + m0
            m_t = jnp.maximum(inter_log, jnp.max(dl, axis=-1, keepdims=True))
            dm = jnp.exp(dl - m_t)
            s = lax.dot_general(q, k, (((1,), (1,)), ((), ())), preferred_element_type=F32)
            p = (s * dm).astype(BF16)
            st = st_ref[h]
            intra = jnp.dot(p, vaug, preferred_element_type=F32)
            inter = jnp.dot(q, st.astype(BF16), preferred_element_type=F32)
            tot = intra + jnp.exp(inter_log - m_t) * inter
            num = tot[:, :ML_V]
            den = tot[:, ML_V:ML_V + 1]
            hv = num / jnp.maximum(jnp.abs(den), jnp.exp(-m_t))
            hn = _rms(hv, gain_ref[:, h * ML_V:(h + 1) * ML_V])
            og = o_ref[pl.ds(r0, L), h * ML_V:(h + 1) * ML_V].astype(F32)
            y_ref[pl.ds(r0, L), h * ML_V:(h + 1) * ML_V] = (_sigmoid(og) * hn).astype(y_ref.dtype)
            b_end = bcol[L - 1:L, :]
            a = b_end - bcol + icol
            g = jnp.max(a, axis=0, keepdims=True)
            wa = jnp.exp(a - g)
            kt = k.astype(F32).T.astype(BF16)
            upd = jnp.dot(kt, (vaug.astype(F32) * wa).astype(BF16), preferred_element_type=F32)
            m_new = jnp.maximum(b_end + m0, g)
            decay = jnp.exp(b_end + m0 - m_new)
            inject = jnp.exp(g - m_new)
            st_ref[h] = decay * st + inject * upd
            m_ref[h] = jnp.broadcast_to(m_new, (SUBLANES, LANES))
        return 0

    lax.fori_loop(0, tm // L, chunk, 0)


def _mlstm(z, zg, ml_norm, B, S):
    tm = min(TILES["ml_tm"], S)
    spb = S // tm
    T = B * S

    def zspec(name):
        w = Z_W[name]
        cb = Z_OFF[name] // w
        return pl.BlockSpec((tm, w), lambda b, i: (b * spb + i, cb))

    return pl.pallas_call(
        _mlstm_kernel,
        out_shape=jax.ShapeDtypeStruct((T, ML_HEADS * ML_V), BF16),
        grid=(B, spb),
        in_specs=[zspec("lq"), zspec("lk"), zspec("lv"), zspec("lo"),
                  pl.BlockSpec((tm, LANES), lambda b, i: (b * spb + i, 0)),
                  pl.BlockSpec((1, ML_HEADS * ML_V), lambda b, i: (0, 0))],
        out_specs=pl.BlockSpec((tm, ML_HEADS * ML_V), lambda b, i: (b * spb + i, 0)),
        scratch_shapes=[pltpu.VMEM((ML_HEADS, ML_QK, ML_VA), F32),
                        pltpu.VMEM((ML_HEADS, SUBLANES, LANES), F32)],
        compiler_params=_cparams(("arbitrary", "arbitrary")),
        name="mlstm",
    )(z, z, z, z, zg, ml_norm.reshape(1, -1))


def _merge_kernel(cb_ref, cc_ref, ch_ref, ccp_ref, chp_ref, su_ref, sv_ref,
                  gc_ref, gs_ref, gm_ref, gl_ref, ymla_ref, yml_ref,
                  convw_ref, sgn_ref, sgw_ref, sgb_ref, wc_ref, ws_ref, wm_ref, wl_ref,
                  o_ref, *, steps_per_batch):
    tm = cb_ref.shape[0]
    i = pl.program_id(0)
    z = cc_ref[...].astype(F32) * ch_ref[...].astype(F32)
    zp = ccp_ref[...].astype(F32) * chp_ref[...].astype(F32)
    zp = jnp.where(i % steps_per_batch == 0, 0.0, zp)
    rid = lax.broadcasted_iota(jnp.int32, (tm, 1), 0)
    z1 = jnp.where(rid == 0, zp[SUBLANES - 1:SUBLANES, :], pltpu.roll(z, 1, 0))
    z2 = jnp.where(rid == 0, zp[SUBLANES - 2:SUBLANES - 1, :],
                   jnp.where(rid == 1, zp[SUBLANES - 1:SUBLANES, :], pltpu.roll(z, 2, 0)))
    cw = convw_ref[...]
    y_conv = cb_ref[...].astype(F32) * (cw[0:1, :] * z2 + cw[1:2, :] * z1 + cw[2:3, :] * z)
    acc = _sigmoid(gc_ref[...].astype(F32)) * jnp.dot(y_conv.astype(BF16), wc_ref[...],
                                                      preferred_element_type=F32)
    u = _gelu_tanh(su_ref[...].astype(F32))
    vn = _rms(_gelu_tanh(sv_ref[...].astype(F32)), sgn_ref[...]).astype(BF16)
    r = lax.broadcasted_iota(jnp.int32, (SG_BLOCK, SG_BLOCK), 0)
    c = lax.broadcasted_iota(jnp.int32, (SG_BLOCK, SG_BLOCK), 1)
    blocks = []
    for n in range(tm // SG_BLOCK):
        cols = []
        for g in range(SG_GROUPS):
            w = jnp.where(c <= r, sgw_ref[g], 0.0).astype(BF16)
            vb = vn[n * SG_BLOCK:(n + 1) * SG_BLOCK, g * SG_GD:(g + 1) * SG_GD]
            cols.append(jnp.dot(w, vb, preferred_element_type=F32) + sgb_ref[:, g:g + 1])
        blocks.append(jnp.concatenate(cols, axis=1))
    mixed = blocks[0] if len(blocks) == 1 else jnp.concatenate(blocks, axis=0)
    y_sg = (u * mixed).astype(BF16)
    acc += _sigmoid(gs_ref[...].astype(F32)) * jnp.dot(y_sg, ws_ref[...], preferred_element_type=F32)
    acc += _sigmoid(gm_ref[...].astype(F32)) * jnp.dot(ymla_ref[...], wm_ref[...], preferred_element_type=F32)
    acc += _sigmoid(gl_ref[...].astype(F32)) * jnp.dot(yml_ref[...], wl_ref[...], preferred_element_type=F32)
    o_ref[...] = acc.astype(o_ref.dtype)


def _merge(z, y_mla, y_ml, conv_w, sg_norm, sg_w, sg_bt, wc, ws, wm, wl, S):
    T = z.shape[0]
    D = wc.shape[1]
    tm = min(TILES["m1_tm"], S)
    spb = S // tm
    rows8 = tm // SUBLANES

    def zspec(name):
        w = Z_W[name]
        cb = Z_OFF[name] // w
        return pl.BlockSpec((tm, w), lambda i: (i, cb))

    def zprev(name):
        w = Z_W[name]
        cb = Z_OFF[name] // w
        return pl.BlockSpec((SUBLANES, w), lambda i: (jnp.maximum(i * rows8 - 1, 0), cb))

    def const(shape):
        nd = len(shape)
        return pl.BlockSpec(shape, lambda i: (0,) * nd, pipeline_mode=pl.Buffered(1))

    return pl.pallas_call(
        functools.partial(_merge_kernel, steps_per_batch=spb),
        out_shape=jax.ShapeDtypeStruct((T, D), BF16),
        grid=(T // tm,),
        in_specs=[zspec("cb"), zspec("cc"), zspec("ch"), zprev("cc"), zprev("ch"), zspec("su"), zspec("sv"),
                  zspec("g_conv"), zspec("g_sg"), zspec("g_mla"), zspec("g_ml"),
                  pl.BlockSpec((tm, y_mla.shape[1]), lambda i: (i, 0)),
                  pl.BlockSpec((tm, y_ml.shape[1]), lambda i: (i, 0)),
                  const(conv_w.shape), const((1, SG_W)), const(sg_w.shape), const(sg_bt.shape),
                  const(wc.shape), const(ws.shape), const(wm.shape), const(wl.shape)],
        out_specs=pl.BlockSpec((tm, D), lambda i: (i, 0)),
        compiler_params=_cparams(("arbitrary",)),
        name="merge",
    )(z, z, z, z, z, z, z, z, z, z, z, y_mla, y_ml,
      conv_w, sg_norm.reshape(1, -1), sg_w, sg_bt, wc, ws, wm, wl)


def _mix_out_kernel(x_ref, m_ref, w_ref, gain_ref, gate_ref, o_ref):
    y = jnp.dot(m_ref[...], w_ref[...], preferred_element_type=F32)
    o_ref[...] = x_ref[...] + gate_ref[0] * _rms(y, gain_ref[...])


def _mix_out(x2, merged, w, gain, gate, S):
    T, D = x2.shape
    tm = min(TILES["m2_tm"], S)
    spb = S // tm
    return pl.pallas_call(
        _mix_out_kernel,
        out_shape=jax.ShapeDtypeStruct((T, D), F32),
        grid=(T // tm,),
        in_specs=[pl.BlockSpec((tm, D), lambda i: (i, 0)),
                  pl.BlockSpec((tm, D), lambda i: (i, 0)),
                  pl.BlockSpec((D, D), lambda i: (0, 0), pipeline_mode=pl.Buffered(1)),
                  pl.BlockSpec((1, D), lambda i: (0, 0)),
                  pl.BlockSpec((1, 1, D), lambda i: (i // spb, 0, 0))],
        out_specs=pl.BlockSpec((tm, D), lambda i: (i, 0)),
        compiler_params=_cparams(("arbitrary",)),
        name="mix_out",
    )(x2, merged, w, gain.reshape(1, D), gate)


def _ffn_kernel(x_ref, g_ref, sc_ref, sh_ref, wg_ref, wu_ref, wd_ref, gpost_ref, gate_ref, o_ref,
                h_ref, acc_ref):
    j = pl.program_id(1)

    @pl.when(j == 0)
    def _():
        y = _rms(x_ref[...], g_ref[...])
        h_ref[...] = (y * (1.0 + sc_ref[0]) + sh_ref[0]).astype(BF16)
        acc_ref[...] = jnp.zeros_like(acc_ref)

    h = h_ref[...]
    g = jnp.dot(h, wg_ref[...], preferred_element_type=F32)
    u = jnp.dot(h, wu_ref[...], preferred_element_type=F32)
    a = (g * _sigmoid(g) * u).astype(BF16)
    acc_ref[...] += jnp.dot(a, wd_ref[...], preferred_element_type=F32)

    @pl.when(j == pl.num_programs(1) - 1)
    def _():
        o_ref[...] = x_ref[...] + gate_ref[0] * _rms(acc_ref[...], gpost_ref[...])


def _ffn_dense(x2, gain_pre, sc, sh, wg, wu, wd, gain_post, gate, S):
    T, D = x2.shape
    F = wg.shape[1]
    tm = min(TILES["ffn_tm"], S)
    tf = min(TILES["ffn_tf"], F)
    spb = S // tm
    return pl.pallas_call(
        _ffn_kernel,
        out_shape=jax.ShapeDtypeStruct((T, D), F32),
        grid=(T // tm, F // tf),
        in_specs=[pl.BlockSpec((tm, D), lambda i, j: (i, 0)),
                  pl.BlockSpec((1, D), lambda i, j: (0, 0)),
                  pl.BlockSpec((1, 1, D), lambda i, j: (i // spb, 0, 0)),
                  pl.BlockSpec((1, 1, D), lambda i, j: (i // spb, 0, 0)),
                  pl.BlockSpec((D, tf), lambda i, j: (0, j)),
                  pl.BlockSpec((D, tf), lambda i, j: (0, j)),
                  pl.BlockSpec((tf, D), lambda i, j: (j, 0)),
                  pl.BlockSpec((1, D), lambda i, j: (0, 0)),
                  pl.BlockSpec((1, 1, D), lambda i, j: (i // spb, 0, 0))],
        out_specs=pl.BlockSpec((tm, D), lambda i, j: (i, 0)),
        scratch_shapes=[pltpu.VMEM((tm, D), BF16), pltpu.VMEM((tm, D), F32)],
        compiler_params=_cparams(("arbitrary", "arbitrary")),
        name="ffn_dense",
    )(x2, gain_pre.reshape(1, D), sc, sh, wg, wu, wd, gain_post.reshape(1, D), gate)


def _router_kernel(x_ref, g_ref, sc_ref, sh_ref, rw_ref, rb_ref, h_ref, meta_ref, wts_ref, cnt_ref, carry_ref):
    tm = x_ref.shape[0]

    @pl.when(pl.program_id(0) == 0)
    def _():
        carry_ref[...] = jnp.zeros_like(carry_ref)

    h = _rms(x_ref[...], g_ref[...]) * (1.0 + sc_ref[0]) + sh_ref[0]
    h_ref[...] = h
    logits = jnp.dot(h, rw_ref[...], precision=lax.Precision.HIGHEST, preferred_element_type=F32) + rb_ref[...]
    lane = lax.broadcasted_iota(jnp.int32, (tm, LANES), 1)
    m1 = jnp.max(logits, axis=-1, keepdims=True)
    i1 = jnp.min(jnp.where(logits == m1, lane, LANES), axis=-1, keepdims=True)
    oh1 = lane == i1
    rest = jnp.where(oh1, 2.0 * NEG_BIG, logits)
    m2 = jnp.max(rest, axis=-1, keepdims=True)
    i2 = jnp.min(jnp.where(rest == m2, lane, LANES), axis=-1, keepdims=True)
    oh2 = lane == i2
    e2 = jnp.exp(m2 - m1)
    den = 1.0 + e2
    w1 = 1.0 / den
    w2 = e2 / den
    oh = jnp.where(oh1 | oh2, 1.0, 0.0)
    r = lax.broadcasted_iota(jnp.int32, (tm, tm), 0)
    c = lax.broadcasted_iota(jnp.int32, (tm, tm), 1)
    strict = jnp.where(c < r, 1.0, 0.0).astype(BF16)
    carry = carry_ref[0:1, :]
    cum = jnp.dot(strict, oh.astype(BF16), preferred_element_type=F32) + carry
    rank1 = jnp.sum(jnp.where(oh1, cum, 0.0), axis=-1, keepdims=True).astype(jnp.int32)
    rank2 = jnp.sum(jnp.where(oh2, cum, 0.0), axis=-1, keepdims=True).astype(jnp.int32)
    new_carry = carry + jnp.sum(oh, axis=0, keepdims=True)
    carry_ref[...] = jnp.broadcast_to(new_carry, carry_ref.shape)
    cnt_ref[...] = jnp.broadcast_to(new_carry, cnt_ref.shape)
    meta_ref[...] = jnp.where(lane == 0, i1, jnp.where(lane == 1, i2, jnp.where(lane == 2, rank1,
                              jnp.where(lane == 3, rank2, 0))))
    wts_ref[...] = jnp.where(lane == 0, w1, jnp.where(lane == 1, w2, 0.0))


def _router(x2, gain, sc, sh, rw, rb, S):
    T, D = x2.shape
    tm = min(TILES["rt_tm"], S)
    spb = S // tm
    return pl.pallas_call(
        _router_kernel,
        out_shape=(jax.ShapeDtypeStruct((T, D), F32),
                   jax.ShapeDtypeStruct((T, LANES), jnp.int32),
                   jax.ShapeDtypeStruct((T, LANES), F32),
                   jax.ShapeDtypeStruct((SUBLANES, LANES), F32)),
        grid=(T // tm,),
        in_specs=[pl.BlockSpec((tm, D), lambda i: (i, 0)),
                  pl.BlockSpec((1, D), lambda i: (0, 0)),
                  pl.BlockSpec((1, 1, D), lambda i: (i // spb, 0, 0)),
                  pl.BlockSpec((1, 1, D), lambda i: (i // spb, 0, 0)),
                  pl.BlockSpec((D, LANES), lambda i: (0, 0)),
                  pl.BlockSpec((1, LANES), lambda i: (0, 0))],
        out_specs=(pl.BlockSpec((tm, D), lambda i: (i, 0)),
                   pl.BlockSpec((tm, LANES), lambda i: (i, 0)),
                   pl.BlockSpec((tm, LANES), lambda i: (i, 0)),
                   pl.BlockSpec((SUBLANES, LANES), lambda i: (0, 0))),
        scratch_shapes=[pltpu.VMEM((SUBLANES, LANES), F32)],
        compiler_params=_cparams(("arbitrary",)),
        name="moe_router",
    )(x2, gain.reshape(1, D), sc, sh, rw, rb)


def _row_copy(src_ref, src_row, dst_ref, dst_row, sem):
    return pltpu.make_async_copy(src_ref.at[pl.ds(src_row, 1)], dst_ref.at[pl.ds(dst_row, 1)], sem)


def _dispatch_kernel(dest_ref, h_ref, xs_in_ref, xs_ref, sem):
    del xs_in_ref
    tm = h_ref.shape[0]

    def issue(r, _):
        for kk in range(TOP_K):
            _row_copy(h_ref, r, xs_ref, dest_ref[0, 0, TOP_K * r + kk], sem).start()
        return 0

    lax.fori_loop(0, tm, issue, 0)

    def drain(r, _):
        for kk in range(TOP_K):
            _row_copy(h_ref, r, xs_ref, dest_ref[0, 0, TOP_K * r + kk], sem).wait()
        return 0

    lax.fori_loop(0, tm, drain, 0)


def _dispatch(h, dest, n_slots):
    T, D = h.shape
    tm = min(TILES["disp_tm"], T)
    nt = T // tm
    xs0 = jnp.zeros((n_slots, D), F32)
    return pl.pallas_call(
        _dispatch_kernel,
        out_shape=jax.ShapeDtypeStruct((n_slots, D), F32),
        grid=(nt,),
        in_specs=[pl.BlockSpec((1, 1, TOP_K * tm), lambda i: (i, 0, 0), memory_space=pltpu.SMEM),
                  pl.BlockSpec((tm, D), lambda i: (i, 0)),
                  pl.BlockSpec(memory_space=pl.ANY)],
        out_specs=pl.BlockSpec(memory_space=pl.ANY),
        scratch_shapes=[pltpu.SemaphoreType.DMA],
        input_output_aliases={2: 0},
        compiler_params=_cparams(("arbitrary",)),
        name="moe_dispatch",
    )(dest.reshape(nt, 1, TOP_K * tm), h, xs0)


def _expert_kernel(be_ref, nu_ref, xs_ref, wg_ref, wu_ref, wd_ref, o_ref, xb_ref, acc_ref):
    i = pl.program_id(0)
    j = pl.program_id(1)
    last = pl.num_programs(1) - 1
    active = i < nu_ref[0]

    @pl.when(active & (j == 0))
    def _():
        xb_ref[...] = xs_ref[...].astype(BF16)
        acc_ref[...] = jnp.zeros_like(acc_ref)

    @pl.when(active)
    def _():
        xb = xb_ref[...]
        g = jnp.dot(xb, wg_ref[0], preferred_element_type=F32)
        u = jnp.dot(xb, wu_ref[0], preferred_element_type=F32)
        a = (g * _sigmoid(g) * u).astype(BF16)
        acc_ref[...] += jnp.dot(a, wd_ref[0], preferred_element_type=F32)

    @pl.when(active & (j == last))
    def _():
        o_ref[...] = acc_ref[...]

    @pl.when(jnp.logical_not(active) & (j == last))
    def _():
        o_ref[...] = jnp.zeros_like(o_ref)


def _experts(xs, block_e, n_used, wg, wu, wd):
    n_slots, D = xs.shape
    E, _, F = wg.shape
    tm = TILES["exp_tm"]
    tf = min(TILES["exp_tf"], F)
    nb = n_slots // tm
    nj = F // tf

    def clamp(i, nu):
        return jnp.minimum(i, nu[0] - 1)

    def x_map(i, j, be, nu):
        return (clamp(i, nu), 0)

    def w_col_map(i, j, be, nu):
        return (be[clamp(i, nu)], 0, jnp.where(i < nu[0], j, nj - 1))

    def w_row_map(i, j, be, nu):
        return (be[clamp(i, nu)], jnp.where(i < nu[0], j, nj - 1), 0)

    return pl.pallas_call(
        _expert_kernel,
        out_shape=jax.ShapeDtypeStruct((n_slots, D), F32),
        grid_spec=pltpu.PrefetchScalarGridSpec(
            num_scalar_prefetch=2,
            grid=(nb, nj),
            in_specs=[pl.BlockSpec((tm, D), x_map),
                      pl.BlockSpec((1, D, tf), w_col_map),
                      pl.BlockSpec((1, D, tf), w_col_map),
                      pl.BlockSpec((1, tf, D), w_row_map)],
            out_specs=pl.BlockSpec((tm, D), lambda i, j, be, nu: (i, 0)),
            scratch_shapes=[pltpu.VMEM((tm, D), BF16), pltpu.VMEM((tm, D), F32)]),
        compiler_params=_cparams(("arbitrary", "arbitrary")),
        name="moe_experts",
    )(block_e, n_used, xs, wg, wu, wd)


def _combine_kernel(dest_ref, x_ref, wts_ref, gain_ref, gate_ref, ys_ref, o_ref, buf_ref, sem):
    tm = x_ref.shape[0]

    def issue(r, _):
        for kk in range(TOP_K):
            _row_copy(ys_ref, dest_ref[0, 0, TOP_K * r + kk], buf_ref.at[kk], r, sem).start()
        return 0

    lax.fori_loop(0, tm, issue, 0)

    def drain(r, _):
        for kk in range(TOP_K):
            _row_copy(ys_ref, dest_ref[0, 0, TOP_K * r + kk], buf_ref.at[kk], r, sem).wait()
        return 0

    lax.fori_loop(0, tm, drain, 0)
    w = wts_ref[...]
    y = w[:, 0:1] * buf_ref[0] + w[:, 1:2] * buf_ref[1]
    o_ref[...] = x_ref[...] + gate_ref[0] * _rms(y, gain_ref[...])


def _combine(x2, ys, dest, wts, gain, gate, S):
    T, D = x2.shape
    tm = min(TILES["disp_tm"], S)
    spb = S // tm
    nt = T // tm
    return pl.pallas_call(
        _combine_kernel,
        out_shape=jax.ShapeDtypeStruct((T, D), F32),
        grid=(nt,),
        in_specs=[pl.BlockSpec((1, 1, TOP_K * tm), lambda i: (i, 0, 0), memory_space=pltpu.SMEM),
                  pl.BlockSpec((tm, D), lambda i: (i, 0)),
                  pl.BlockSpec((tm, LANES), lambda i: (i, 0)),
                  pl.BlockSpec((1, D), lambda i: (0, 0)),
                  pl.BlockSpec((1, 1, D), lambda i: (i // spb, 0, 0)),
                  pl.BlockSpec(memory_space=pl.ANY)],
        out_specs=pl.BlockSpec((tm, D), lambda i: (i, 0)),
        scratch_shapes=[pltpu.VMEM((TOP_K, tm, D), F32), pltpu.SemaphoreType.DMA],
        compiler_params=_cparams(("arbitrary",)),
        name="moe_combine",
    )(dest.reshape(nt, 1, TOP_K * tm), x2, wts, gain.reshape(1, D), gate, ys)


def _moe(x2, gain_pre, sc, sh, router_w, router_b, wg, wu, wd, gain_post, gate, S):
    T, D = x2.shape
    E = router_w.shape[1]
    tm_e = TILES["exp_tm"]
    rw = jnp.zeros((D, LANES), F32).at[:, :E].set(router_w)
    rb = jnp.full((1, LANES), NEG_BIG, F32).at[0, :E].set(router_b)
    h, meta, wts, cnt = _router(x2, gain_pre, sc, sh, rw, rb, S)
    counts = cnt[0, :E].astype(jnp.int32)
    padded = ((counts + tm_e - 1) // tm_e) * tm_e
    pend = jnp.cumsum(padded)
    pstart = pend - padded
    dest = pstart[meta[:, 0:TOP_K]] + meta[:, TOP_K:2 * TOP_K]
    n_slots = T * TOP_K + E * tm_e
    nb = n_slots // tm_e
    block_e = jnp.minimum(jnp.searchsorted(pend, jnp.arange(nb, dtype=jnp.int32) * tm_e, side="right"),
                          E - 1).astype(jnp.int32)
    n_used = (pend[-1:] // tm_e).astype(jnp.int32)
    xs = _dispatch(h, dest.reshape(-1), n_slots)
    ys = _experts(xs, block_e, n_used, wg, wu, wd)
    return _combine(x2, ys, dest.reshape(-1), wts, gain_post, gate, S)


def _prep_in_proj(w_in, b_in):
    q_scale = np.float32((MLA_NOPE + MLA_ROPE) ** -0.5 * np.log2(np.e))
    k_scale = np.float32(ML_QK ** -0.5)

    def build(a):
        lead = a.shape[:-1]

        def seg(name):
            s = IN_START[name]
            return a[..., s:s + IN_SIZE[name]]

        mq = seg("mq").reshape(lead + (MLA_HEADS, MLA_NOPE + MLA_ROPE)) * q_scale
        parts = {
            "qn": mq[..., :MLA_NOPE].reshape(lead + (MLA_HEADS * MLA_NOPE,)),
            "qr": mq[..., MLA_NOPE:].reshape(lead + (MLA_HEADS * MLA_ROPE,)),
            "kr": jnp.concatenate([seg("mkr"), jnp.zeros(lead + (LANES - MLA_ROPE,), a.dtype)], axis=-1),
            "gates": jnp.concatenate([seg("li"), seg("lf"),
                                      jnp.zeros(lead + (LANES - 2 * ML_HEADS,), a.dtype)], axis=-1),
            "lk": seg("lk") * k_scale,
            "ckv": seg("mckv"),
            "pad": jnp.zeros(lead + (Z_W["pad"],), a.dtype),
        }
        cols = [parts[n] if n in parts else seg(n) for n, _ in Z_ORDER]
        return jnp.concatenate(cols, axis=-1)

    return build(w_in).astype(BF16), build(b_in)


def kernel(x, c, positions, ada_w, ada_b, norm_pre_mix, norm_post_mix, norm_pre_ffn, norm_post_ffn, w_in, b_in,
           conv_w, sg_norm, sg_w, sg_b, mla_kv_norm, mla_w_uk, mla_w_uv, ml_norm, w_conv_out, w_sg_out,
           w_mla_out, w_ml_out, w_mix_out, ffn_w_gate, ffn_w_up, ffn_w_down, router_w, router_b,
           exp_w_gate, exp_w_up, exp_w_down):
    B, S, D = x.shape
    T = B * S
    x2 = x.reshape(T, D)
    pos = positions.reshape(T, 1).astype(jnp.int32)
    half = MLA_ROPE // 2
    freq32 = ROPE_THETA ** (-jnp.arange(half, dtype=F32) / half)
    freq = jnp.tile(freq32, LANES // half).reshape(1, LANES)
    mod = _ada_mod(c, ada_w, ada_b)
    for l in range(DEPTH):
        sh1, sc1, g1, sh2, sc2, g2 = [m.reshape(B, 1, D) for m in jnp.split(mod[l], 6, axis=-1)]
        w_p, b_p = _prep_in_proj(w_in[l], b_in[l])
        z, zg = _in_proj(x2, norm_pre_mix[l], sc1, sh1, w_p, b_p, S)
        q, k, v = _mla_prep(z, pos, freq, mla_kv_norm[l],
                            mla_w_uk[l].reshape(MLA_KV_RANK, -1).T.astype(BF16),
                            mla_w_uv[l].reshape(MLA_KV_RANK, -1).astype(BF16), B, S)
        y_mla = _attention(q, k, v).reshape(T, MLA_HEADS * MLA_V)
        y_ml = _mlstm(z, zg, ml_norm[l], B, S)
        merged = _merge(z, y_mla, y_ml, conv_w[l], sg_norm[l], sg_w[l], sg_b[l].T,
                        w_conv_out[l].astype(BF16), w_sg_out[l].astype(BF16),
                        w_mla_out[l].astype(BF16), w_ml_out[l].astype(BF16), S)
        x2 = _mix_out(x2, merged, w_mix_out[l].astype(BF16), norm_post_mix[l], g1, S)
        if l % 2 == 0:
            x2 = _ffn_dense(x2, norm_pre_ffn[l], sc2, sh2, ffn_w_gate[l // 2].astype(BF16),
                            ffn_w_up[l // 2].astype(BF16), ffn_w_down[l // 2].astype(BF16),
                            norm_post_ffn[l], g2, S)
        else:
            x2 = _moe(x2, norm_pre_ffn[l], sc2, sh2, router_w[l // 2], router_b[l // 2],
                      exp_w_gate[l // 2].astype(BF16), exp_w_up[l // 2].astype(BF16),
                      exp_w_down[l // 2].astype(BF16), norm_post_ffn[l], g2, S)
    return x2.reshape(B, S, D)
```

```python
import functools

import jax
import jax.numpy as jnp
import numpy as np
from jax import lax
from jax.experimental import pallas as pl
from jax.experimental.pallas import tpu as pltpu

F32 = jnp.float32
BF16 = jnp.bfloat16

D_MODEL = 2048
DEPTH = 2
CHUNK = 64
EPS = 1e-6
CONV_W = 1024
CONV_K = 3
SG_W = 1024
SG_BLOCK = 128
SG_GROUPS = 8
SG_GD = SG_W // SG_GROUPS
MLA_HEADS = 16
MLA_NOPE = 128
MLA_ROPE = 64
MLA_V = 128
MLA_KV_RANK = 512
ROPE_THETA = 10000.0
ML_HEADS = 4
ML_QK = 128
ML_V = 256
D_FF = 5632
N_EXPERTS = 8
TOP_K = 2
D_FF_EXPERT = 7168
IN_SIZES = (
    CONV_W, CONV_W, CONV_W, SG_W, SG_W,
    MLA_HEADS * (MLA_NOPE + MLA_ROPE), MLA_KV_RANK, MLA_ROPE,
    ML_HEADS * ML_QK, ML_HEADS * ML_QK, ML_HEADS * ML_V, ML_HEADS * ML_V, ML_HEADS, ML_HEADS,
    D_MODEL, D_MODEL, D_MODEL, D_MODEL,
)
IN_NAMES = ("cb", "cc", "ch", "su", "sv", "mq", "mckv", "mkr", "lq", "lk", "lv", "lo", "li", "lf",
            "g_conv", "g_sg", "g_mla", "g_ml")
IN_START = {n: sum(IN_SIZES[:j]) for j, n in enumerate(IN_NAMES)}
IN_SIZE = dict(zip(IN_NAMES, IN_SIZES))

LANES = 128
SUBLANES = 8
VMEM_LIMIT = 56 * 1024 * 1024
NEG_BIG = -1e30

Z_ORDER = (("qn", 2048), ("g_conv", 2048), ("g_sg", 2048), ("g_mla", 2048), ("g_ml", 2048),
           ("cb", 1024), ("cc", 1024), ("ch", 1024), ("su", 1024), ("sv", 1024), ("qr", 1024),
           ("lv", 1024), ("lo", 1024), ("ckv", 512), ("lq", 512), ("lk", 512),
           ("kr", 128), ("gates", 128), ("pad", 256))
Z_OFF = {}
_o = 0
for _n, _w in Z_ORDER:
    Z_OFF[_n] = _o
    _o += _w
NZ = _o
Z_W = dict(Z_ORDER)

TILES = dict(
    in_tm=1024, in_tn=2048,
    prep_tm=512,
    attn_tq=1024,
    ml_tm=512, ml_chunk=128,
    m1_tm=256,
    m2_tm=512,
    ffn_tm=512, ffn_tf=512,
    rt_tm=512,
    disp_tm=256,
    exp_tm=1024, exp_tf=512,
    ada_tn=1024,
)


def _cparams(sem, vmem=VMEM_LIMIT):
    return pltpu.CompilerParams(dimension_semantics=sem, vmem_limit_bytes=vmem)


def _sigmoid(x):
    return 1.0 / (1.0 + jnp.exp(-x))


def _gelu_tanh(x):
    return 0.5 * x * (1.0 + jnp.tanh(np.float32(np.sqrt(2.0 / np.pi)) * (x + 0.044715 * (x * x * x))))


def _rms(x, gain):
    return x * lax.rsqrt(jnp.mean(x * x, axis=-1, keepdims=True) + EPS) * gain


def _ada_kernel(c_ref, w_ref, b_ref, o_ref):
    c = c_ref[...]
    cs = (c * _sigmoid(c)).astype(BF16)
    o_ref[0] = jnp.dot(cs, w_ref[0].astype(BF16), preferred_element_type=F32) + b_ref[0]


def _ada_mod(c, ada_w, ada_b):
    L, D, N = ada_w.shape
    B = c.shape[0]
    tn = min(TILES["ada_tn"], N)
    c_pad = jnp.zeros((SUBLANES, D), F32).at[:B].set(c)
    out = pl.pallas_call(
        _ada_kernel,
        out_shape=jax.ShapeDtypeStruct((L, SUBLANES, N), F32),
        grid=(L, N // tn),
        in_specs=[pl.BlockSpec((SUBLANES, D), lambda l, j: (0, 0)),
                  pl.BlockSpec((1, D, tn), lambda l, j: (l, 0, j)),
                  pl.BlockSpec((1, 1, tn), lambda l, j: (l, 0, j))],
        out_specs=pl.BlockSpec((1, SUBLANES, tn), lambda l, j: (l, 0, j)),
        compiler_params=_cparams(("arbitrary", "arbitrary")),
        name="ada_mod",
    )(c_pad, ada_w, ada_b.reshape(L, 1, N))
    return out[:, :B]


def _in_proj_kernel(x_ref, g_ref, sc_ref, sh_ref, w_ref, b_ref, o_ref, og_ref, h_ref, *, gates_tile, gates_off):
    tm, tn = o_ref.shape
    slab = min(ROW_SLAB, tm)

    @pl.when(pl.program_id(1) == 0)
    def _():
        for r in range(tm // slab):
            rows = slice(r * slab, (r + 1) * slab)
            y = _rms(x_ref[rows, :], g_ref[...])
            h_ref[rows, :] = (y * (1.0 + sc_ref[0]) + sh_ref[0]).astype(BF16)

    h = h_ref[...]
    step = min(DOWN_PROJ_COLS, tn)
    for c in range(tn // step):
        cols = slice(c * step, (c + 1) * step)
        acc = jnp.dot(h, w_ref[:, cols], preferred_element_type=F32) + b_ref[:, cols]
        o_ref[:, cols] = acc.astype(o_ref.dtype)
        if c == gates_off // step:
            @pl.when(pl.program_id(1) == gates_tile)
            def _(acc=acc):
                og_ref[...] = acc[:, gates_off % step:gates_off % step + LANES]


def _in_proj(x2, gain, sc, sh, w, b, S):
    T, D = x2.shape
    N = w.shape[1]
    tm = min(TILES["in_tm"], S)
    tn = min(TILES["in_tn"], N)
    spb = S // tm
    kern = functools.partial(_in_proj_kernel, gates_tile=Z_OFF["gates"] // tn, gates_off=Z_OFF["gates"] % tn)
    return pl.pallas_call(
        kern,
        out_shape=(jax.ShapeDtypeStruct((T, N), BF16), jax.ShapeDtypeStruct((T, LANES), F32)),
        grid=(T // tm, N // tn),
        in_specs=[pl.BlockSpec((tm, D), lambda i, j: (i, 0)),
                  pl.BlockSpec((1, D), lambda i, j: (0, 0)),
                  pl.BlockSpec((1, 1, D), lambda i, j: (i // spb, 0, 0)),
                  pl.BlockSpec((1, 1, D), lambda i, j: (i // spb, 0, 0)),
                  pl.BlockSpec((D, tn), lambda i, j: (0, j)),
                  pl.BlockSpec((1, tn), lambda i, j: (0, j))],
        out_specs=(pl.BlockSpec((tm, tn), lambda i, j: (i, j)),
                   pl.BlockSpec((tm, LANES), lambda i, j: (i, 0))),
        scratch_shapes=[pltpu.VMEM((tm, D), BF16)],
        compiler_params=_cparams(("arbitrary", "arbitrary")),
        name="in_proj",
    )(x2, gain.reshape(1, D), sc, sh, w, b.reshape(1, N))


def _swap_halves(v, first_half):
    return jnp.where(first_half, pltpu.roll(v, LANES - MLA_ROPE // 2, 1), pltpu.roll(v, MLA_ROPE // 2, 1))


def _mla_prep_kernel(qn_ref, qr_ref, ckv_ref, kr_ref, pos_ref, freq_ref, kvn_ref, wuk_ref, wuv_ref,
                     q_ref, k_ref, v_ref):
    tm = qn_ref.shape[0]
    ang = pos_ref[...].astype(F32) * freq_ref[...]
    cos = jnp.cos(ang)
    sin = jnp.sin(ang)
    lane = lax.broadcasted_iota(jnp.int32, (1, LANES), 1)
    first_half = (lane % MLA_ROPE) < (MLA_ROPE // 2)
    sgn_sin = jnp.where(first_half, -sin, sin)

    for c in range(MLA_HEADS // 2):
        v = qr_ref[:, c * LANES:(c + 1) * LANES].astype(F32)
        r = (v * cos + _swap_halves(v, first_half) * sgn_sin).astype(BF16)
        for h in (2 * c, 2 * c + 1):
            q_ref[0, h, :, 0:MLA_NOPE] = qn_ref[:, h * MLA_NOPE:(h + 1) * MLA_NOPE]
            q_ref[0, h, :, MLA_NOPE:2 * MLA_NOPE] = r

    cn = _rms(ckv_ref[...].astype(F32), kvn_ref[...]).astype(BF16)
    knt = lax.dot_general(wuk_ref[...], cn, (((1,), (1,)), ((), ())), preferred_element_type=F32).astype(BF16)
    vv = jnp.dot(cn, wuv_ref[...], preferred_element_type=F32).astype(BF16)
    ka = kr_ref[...].astype(F32)
    kr_even = ka * cos + _swap_halves(ka, first_half) * sgn_sin
    kr_odd = pltpu.roll(kr_even, MLA_ROPE, 1)
    krt_even = kr_even.T.astype(BF16)
    krt_odd = kr_odd.T.astype(BF16)
    ones_col = jnp.where(lax.broadcasted_iota(jnp.int32, (tm, LANES), 1) == 0, 1.0, 0.0).astype(BF16)
    for h in range(MLA_HEADS):
        k_ref[0, h, 0:MLA_NOPE, :] = knt[h * MLA_NOPE:(h + 1) * MLA_NOPE, :]
        k_ref[0, h, MLA_NOPE:2 * MLA_NOPE, :] = krt_even if h % 2 == 0 else krt_odd
        v_ref[0, h, :, 0:MLA_V] = vv[:, h * MLA_V:(h + 1) * MLA_V]
        v_ref[0, h, :, MLA_V:MLA_V + LANES] = ones_col


def _mla_prep(z, pos, freq, kv_norm, w_uk_t, w_uv, B, S):
    tm = min(TILES["prep_tm"], S)
    spb = S // tm
    H = MLA_HEADS
    KD = 2 * MLA_NOPE

    def zspec(name):
        w = Z_W[name]
        cb = Z_OFF[name] // w
        return pl.BlockSpec((tm, w), lambda b, i: (b * spb + i, cb))

    return pl.pallas_call(
        _mla_prep_kernel,
        out_shape=(jax.ShapeDtypeStruct((B, H, S, KD), BF16),
                   jax.ShapeDtypeStruct((B, H, KD, S), BF16),
                   jax.ShapeDtypeStruct((B, H, S, MLA_V + LANES), BF16)),
        grid=(B, spb),
        in_specs=[zspec("qn"), zspec("qr"), zspec("ckv"), zspec("kr"),
                  pl.BlockSpec((tm, 1), lambda b, i: (b * spb + i, 0)),
                  pl.BlockSpec((1, LANES), lambda b, i: (0, 0)),
                  pl.BlockSpec((1, MLA_KV_RANK), lambda b, i: (0, 0)),
                  pl.BlockSpec((H * MLA_NOPE, MLA_KV_RANK), lambda b, i: (0, 0)),
                  pl.BlockSpec((MLA_KV_RANK, H * MLA_V), lambda b, i: (0, 0))],
        out_specs=(pl.BlockSpec((1, H, tm, KD), lambda b, i: (b, 0, i, 0)),
                   pl.BlockSpec((1, H, KD, tm), lambda b, i: (b, 0, 0, i)),
                   pl.BlockSpec((1, H, tm, MLA_V + LANES), lambda b, i: (b, 0, i, 0))),
        compiler_params=_cparams(("arbitrary", "arbitrary")),
        name="mla_prep",
    )(z, z, z, z, pos, freq, kv_norm.reshape(1, -1), w_uk_t, w_uv)


ATTN_GROUPS = 2
ATTN_STRIP = 32
ATTN_KTILE = 256


def _attn_kernel(q_ref, kt_ref, v_ref, o_ref, s0_ref, s1_ref, *scratch):
    G = ATTN_GROUPS
    p_refs, acc_refs, m_refs, bm_refs, al_refs = (scratch[n * G:(n + 1) * G] for n in range(5))
    tq = q_ref.shape[2]
    tk = tq // G
    va = v_ref.shape[3]
    i = pl.program_id(2)
    for g in range(G):
        m_refs[g][...] = jnp.full(m_refs[g].shape, NEG_BIG, F32)
        acc_refs[g][...] = jnp.zeros(acc_refs[g].shape, F32)

    def scores(blk, s_ref, first_row=0):
        start = pl.multiple_of(blk * tk, tk)
        s_ref[first_row:, :] = jnp.dot(q_ref[0, 0, first_row:, :], kt_ref[0, 0, :, pl.ds(start, tk)],
                                       preferred_element_type=F32)

    def softmax_pv(blk, s_ref, modes):
        v = v_ref[0, 0, pl.ds(pl.multiple_of(blk * tk, tk), tk), :]
        for g in range(G):
            if modes[g] == "skip":
                continue
            p_ref, acc_ref, m_ref, bm_ref, al_ref = p_refs[g], acc_refs[g], m_refs[g], bm_refs[g], al_refs[g]

            def load_strip(r, cols=slice(0, tk)):
                x = s_ref[g * tk + r * ATTN_STRIP:g * tk + (r + 1) * ATTN_STRIP, cols]
                if modes[g] == "diag":
                    row = (lax.broadcasted_iota(jnp.int32, x.shape, 0) + r * ATTN_STRIP) // CHUNK
                    col = (lax.broadcasted_iota(jnp.int32, x.shape, 1) + cols.start) // CHUNK
                    x = jnp.where(col <= row, x, NEG_BIG)
                return x

            for r in range(tk // ATTN_STRIP):
                rows = slice(r * ATTN_STRIP, (r + 1) * ATTN_STRIP)
                bm_ref[rows, :] = jnp.broadcast_to(jnp.max(load_strip(r), axis=-1, keepdims=True),
                                                   (ATTN_STRIP, LANES))
            m_old = m_ref[...]
            m_new = jnp.maximum(m_old, bm_ref[...])
            m_ref[...] = m_new
            al_ref[...] = jnp.exp2(m_old - m_new)
            pv = acc_ref[...] * jnp.concatenate([al_ref[...]] * (va // LANES), axis=1)
            kt = min(ATTN_KTILE, tk)
            for c in range(tk // kt):
                cols = slice(c * kt, (c + 1) * kt)
                for r in range(tk // ATTN_STRIP):
                    rows = slice(r * ATTN_STRIP, (r + 1) * ATTN_STRIP)
                    p = jnp.exp2(load_strip(r, cols) - jnp.concatenate([m_ref[rows, :]] * (kt // LANES), axis=1))
                    p_ref[rows, cols] = p.astype(BF16)
                pv = pv + jnp.dot(p_ref[:, cols], v[cols, :], preferred_element_type=F32)
            acc_ref[...] = pv

    scores(0, s0_ref)

    def body(t, carry):
        scores(2 * t + 1, s1_ref)
        softmax_pv(2 * t, s0_ref, ("full", "full"))
        scores(2 * t + 2, s0_ref)
        softmax_pv(2 * t + 1, s1_ref, ("full", "full"))
        return carry

    lax.fori_loop(0, i, body, 0)
    scores(2 * i + 1, s1_ref, first_row=tk)
    softmax_pv(2 * i, s0_ref, ("diag", "full"))
    softmax_pv(2 * i + 1, s1_ref, ("skip", "diag"))

    for g in range(G):
        acc = acc_refs[g][...]
        o_ref[0, g * tk:(g + 1) * tk, :] = (acc[:, :MLA_V] / acc[:, MLA_V:MLA_V + 1]).astype(o_ref.dtype)


def _attention(q, kt, v):
    B, H, S, KD = q.shape
    VA = v.shape[3]
    tq = min(TILES["attn_tq"], S)
    hq = tq // ATTN_GROUPS
    assert hq % CHUNK == 0 and hq % ATTN_STRIP == 0 and S % tq == 0
    return pl.pallas_call(
        _attn_kernel,
        out_shape=jax.ShapeDtypeStruct((B, S, H * MLA_V), BF16),
        grid=(B, H, S // tq),
        in_specs=[pl.BlockSpec((1, 1, tq, KD), lambda b, h, i: (b, h, i, 0)),
                  pl.BlockSpec((1, 1, KD, S), lambda b, h, i: (b, h, 0, 0)),
                  pl.BlockSpec((1, 1, S, VA), lambda b, h, i: (b, h, 0, 0))],
        out_specs=pl.BlockSpec((1, tq, MLA_V), lambda b, h, i: (b, i, h)),
        scratch_shapes=([pltpu.VMEM((tq, hq), F32)] * 2 + [pltpu.VMEM((hq, hq), BF16)] * ATTN_GROUPS
                        + [pltpu.VMEM((hq, VA), F32)] * ATTN_GROUPS
                        + [pltpu.VMEM((hq, LANES), F32)] * (3 * ATTN_GROUPS)),
        compiler_params=_cparams(("arbitrary", "arbitrary", "arbitrary")),
        name="mla_attention",
    )(q, kt, v)


ML_VA = ML_V + LANES


def _log_sigmoid(x):
    return -(jnp.maximum(-x, 0.0) + jnp.log1p(jnp.exp(-jnp.abs(x))))


def _mlstm_kernel(q_ref, k_ref, v_ref, o_ref, g_ref, gain_ref, y_ref, st_ref, m_ref):
    L = TILES["ml_chunk"]
    tm = q_ref.shape[0]
    assert L == LANES and tm % L == 0

    @pl.when(pl.program_id(1) == 0)
    def _():
        st_ref[...] = jnp.zeros_like(st_ref)
        m_ref[...] = jnp.zeros_like(m_ref)

    row = lax.broadcasted_iota(jnp.int32, (L, L), 0)
    col = lax.broadcasted_iota(jnp.int32, (L, L), 1)
    causal = col <= row
    tril = jnp.where(causal, 1.0, 0.0).astype(F32)
    triu = jnp.where(row <= col, 1.0, 0.0).astype(F32)
    lane_a = lax.broadcasted_iota(jnp.int32, (L, LANES), 1)
    ones_col = jnp.where(lane_a == 0, 1.0, 0.0).astype(BF16)

    def chunk(c, _):
        r0 = pl.multiple_of(c * L, L)
        gf = g_ref[pl.ds(r0, L), :]
        gt = gf.T
        lf_c = _log_sigmoid(gf)
        b_c = jnp.dot(tril, lf_c, precision=lax.Precision.HIGHEST, preferred_element_type=F32)
        lf_r = _log_sigmoid(gt)
        b_r = jnp.dot(lf_r, triu, precision=lax.Precision.HIGHEST, preferred_element_type=F32)
        for h in range(ML_HEADS):
            q = q_ref[pl.ds(r0, L), h * ML_QK:(h + 1) * ML_QK]
            k = k_ref[pl.ds(r0, L), h * ML_QK:(h + 1) * ML_QK]
            v = v_ref[pl.ds(r0, L), h * ML_V:(h + 1) * ML_V]
            vaug = jnp.concatenate([v, ones_col], axis=1)
            bcol = b_c[:, ML_HEADS + h:ML_HEADS + h + 1]
            icol = gf[:, h:h + 1]
            brow = b_r[ML_HEADS + h:ML_HEADS + h + 1, :]
            irow = gt[h:h + 1, :]
            m0 = m_ref[h][0:1, 0:1]
            dl = jnp.where(causal, bcol - brow + irow, NEG_BIG)
            inter_log = bcol + m0
            m_t = jnp.maximum(inter_log, jnp.max(dl, axis=-1, keepdims=True))
            dm = jnp.exp(dl - m_t)
            s = lax.dot_general(q, k, (((1,), (1,)), ((), ())), preferred_element_type=F32)
            p = (s * dm).astype(BF16)
            st = st_ref[h]
            intra = jnp.dot(p, vaug, preferred_element_type=F32)
            inter = jnp.dot(q, st.astype(BF16), preferred_element_type=F32)
            tot = intra + jnp.exp(inter_log - m_t) * inter
            num = tot[:, :ML_V]
            den = tot[:, ML_V:ML_V + 1]
            hv = num / jnp.maximum(jnp.abs(den), jnp.exp(-m_t))
            hn = _rms(hv, gain_ref[:, h * ML_V:(h + 1) * ML_V])
            og = o_ref[pl.ds(r0, L), h * ML_V:(h + 1) * ML_V].astype(F32)
            y_ref[pl.ds(r0, L), h * ML_V:(h + 1) * ML_V] = (_sigmoid(og) * hn).astype(y_ref.dtype)
            b_end = bcol[L - 1:L, :]
            a = b_end - bcol + icol
            g = jnp.max(a, axis=0, keepdims=True)
            wa = jnp.exp(a - g)
            kt = k.astype(F32).T.astype(BF16)
            upd = jnp.dot(kt, (vaug.astype(F32) * wa).astype(BF16), preferred_element_type=F32)
            m_new = jnp.maximum(b_end + m0, g)
            decay = jnp.exp(b_end + m0 - m_new)
            inject = jnp.exp(g - m_new)
            st_ref[h] = decay * st + inject * upd
            m_ref[h] = jnp.broadcast_to(m_new, (SUBLANES, LANES))
        return 0

    lax.fori_loop(0, tm // L, chunk, 0)


def _mlstm(z, zg, ml_norm, B, S):
    tm = min(TILES["ml_tm"], S)
    spb = S // tm
    T = B * S

    def zspec(name):
        w = Z_W[name]
        cb = Z_OFF[name] // w
        return pl.BlockSpec((tm, w), lambda b, i: (b * spb + i, cb))

    return pl.pallas_call(
        _mlstm_kernel,
        out_shape=jax.ShapeDtypeStruct((T, ML_HEADS * ML_V), BF16),
        grid=(B, spb),
        in_specs=[zspec("lq"), zspec("lk"), zspec("lv"), zspec("lo"),
                  pl.BlockSpec((tm, LANES), lambda b, i: (b * spb + i, 0)),
                  pl.BlockSpec((1, ML_HEADS * ML_V), lambda b, i: (0, 0))],
        out_specs=pl.BlockSpec((tm, ML_HEADS * ML_V), lambda b, i: (b * spb + i, 0)),
        scratch_shapes=[pltpu.VMEM((ML_HEADS, ML_QK, ML_VA), F32),
                        pltpu.VMEM((ML_HEADS, SUBLANES, LANES), F32)],
        compiler_params=_cparams(("arbitrary", "arbitrary")),
        name="mlstm",
    )(z, z, z, z, zg, ml_norm.reshape(1, -1))


def _merge_kernel(cb_ref, cc_ref, ch_ref, ccp_ref, chp_ref, su_ref, sv_ref,
                  gc_ref, gs_ref, gm_ref, gl_ref, ymla_ref, yml_ref,
                  convw_ref, sgn_ref, sgw_ref, sgb_ref, wc_ref, ws_ref, wm_ref, wl_ref,
                  o_ref, *, steps_per_batch):
    tm = cb_ref.shape[0]
    i = pl.program_id(0)
    z = cc_ref[...].astype(F32) * ch_ref[...].astype(F32)
    zp = ccp_ref[...].astype(F32) * chp_ref[...].astype(F32)
    zp = jnp.where(i % steps_per_batch == 0, 0.0, zp)
    rid = lax.broadcasted_iota(jnp.int32, (tm, 1), 0)
    z1 = jnp.where(rid == 0, zp[SUBLANES - 1:SUBLANES, :], pltpu.roll(z, 1, 0))
    z2 = jnp.where(rid == 0, zp[SUBLANES - 2:SUBLANES - 1, :],
                   jnp.where(rid == 1, zp[SUBLANES - 1:SUBLANES, :], pltpu.roll(z, 2, 0)))
    cw = convw_ref[...]
    y_conv = cb_ref[...].astype(F32) * (cw[0:1, :] * z2 + cw[1:2, :] * z1 + cw[2:3, :] * z)
    acc = _sigmoid(gc_ref[...].astype(F32)) * jnp.dot(y_conv.astype(BF16), wc_ref[...],
                                                      preferred_element_type=F32)
    u = _gelu_tanh(su_ref[...].astype(F32))
    vn = _rms(_gelu_tanh(sv_ref[...].astype(F32)), sgn_ref[...]).astype(BF16)
    r = lax.broadcasted_iota(jnp.int32, (SG_BLOCK, SG_BLOCK), 0)
    c = lax.broadcasted_iota(jnp.int32, (SG_BLOCK, SG_BLOCK), 1)
    blocks = []
    for n in range(tm // SG_BLOCK):
        cols = []
        for g in range(SG_GROUPS):
            w = jnp.where(c <= r, sgw_ref[g], 0.0).astype(BF16)
            vb = vn[n * SG_BLOCK:(n + 1) * SG_BLOCK, g * SG_GD:(g + 1) * SG_GD]
            cols.append(jnp.dot(w, vb, preferred_element_type=F32) + sgb_ref[:, g:g + 1])
        blocks.append(jnp.concatenate(cols, axis=1))
    mixed = blocks[0] if len(blocks) == 1 else jnp.concatenate(blocks, axis=0)
    y_sg = (u * mixed).astype(BF16)
    acc += _sigmoid(gs_ref[...].astype(F32)) * jnp.dot(y_sg, ws_ref[...], preferred_element_type=F32)
    acc += _sigmoid(gm_ref[...].astype(F32)) * jnp.dot(ymla_ref[...], wm_ref[...], preferred_element_type=F32)
    acc += _sigmoid(gl_ref[...].astype(F32)) * jnp.dot(yml_ref[...], wl_ref[...], preferred_element_type=F32)
    o_ref[...] = acc.astype(o_ref.dtype)


def _merge(z, y_mla, y_ml, conv_w, sg_norm, sg_w, sg_bt, wc, ws, wm, wl, S):
    T = z.shape[0]
    D = wc.shape[1]
    tm = min(TILES["m1_tm"], S)
    spb = S // tm
    rows8 = tm // SUBLANES

    def zspec(name):
        w = Z_W[name]
        cb = Z_OFF[name] // w
        return pl.BlockSpec((tm, w), lambda i: (i, cb))

    def zprev(name):
        w = Z_W[name]
        cb = Z_OFF[name] // w
        return pl.BlockSpec((SUBLANES, w), lambda i: (jnp.maximum(i * rows8 - 1, 0), cb))

    def const(shape):
        nd = len(shape)
        return pl.BlockSpec(shape, lambda i: (0,) * nd, pipeline_mode=pl.Buffered(1))

    return pl.pallas_call(
        functools.partial(_merge_kernel, steps_per_batch=spb),
        out_shape=jax.ShapeDtypeStruct((T, D), BF16),
        grid=(T // tm,),
        in_specs=[zspec("cb"), zspec("cc"), zspec("ch"), zprev("cc"), zprev("ch"), zspec("su"), zspec("sv"),
                  zspec("g_conv"), zspec("g_sg"), zspec("g_mla"), zspec("g_ml"),
                  pl.BlockSpec((tm, y_mla.shape[1]), lambda i: (i, 0)),
                  pl.BlockSpec((tm, y_ml.shape[1]), lambda i: (i, 0)),
                  const(conv_w.shape), const((1, SG_W)), const(sg_w.shape), const(sg_bt.shape),
                  const(wc.shape), const(ws.shape), const(wm.shape), const(wl.shape)],
        out_specs=pl.BlockSpec((tm, D), lambda i: (i, 0)),
        compiler_params=_cparams(("arbitrary",)),
        name="merge",
    )(z, z, z, z, z, z, z, z, z, z, z, y_mla, y_ml,
      conv_w, sg_norm.reshape(1, -1), sg_w, sg_bt, wc, ws, wm, wl)


def _mix_out_kernel(x_ref, m_ref, w_ref, gain_ref, gate_ref, o_ref):
    y = jnp.dot(m_ref[...], w_ref[...], preferred_element_type=F32)
    o_ref[...] = x_ref[...] + gate_ref[0] * _rms(y, gain_ref[...])


def _mix_out(x2, merged, w, gain, gate, S):
    T, D = x2.shape
    tm = min(TILES["m2_tm"], S)
    spb = S // tm
    return pl.pallas_call(
        _mix_out_kernel,
        out_shape=jax.ShapeDtypeStruct((T, D), F32),
        grid=(T // tm,),
        in_specs=[pl.BlockSpec((tm, D), lambda i: (i, 0)),
                  pl.BlockSpec((tm, D), lambda i: (i, 0)),
                  pl.BlockSpec((D, D), lambda i: (0, 0), pipeline_mode=pl.Buffered(1)),
                  pl.BlockSpec((1, D), lambda i: (0, 0)),
                  pl.BlockSpec((1, 1, D), lambda i: (i // spb, 0, 0))],
        out_specs=pl.BlockSpec((tm, D), lambda i: (i, 0)),
        compiler_params=_cparams(("arbitrary",)),
        name="mix_out",
    )(x2, merged, w, gain.reshape(1, D), gate)


DOWN_PROJ_COLS = 512
ROW_SLAB = 256


def _accumulate_dot(o_ref, a, w_ref):
    n = o_ref.shape[1]
    step = min(DOWN_PROJ_COLS, n)
    for c in range(n // step):
        cols = slice(c * step, (c + 1) * step)
        w = w_ref[:, cols] if len(w_ref.shape) == 2 else w_ref[0, :, cols]
        o_ref[:, cols] += jnp.dot(a, w, preferred_element_type=F32)


def _ffn_kernel(x_ref, g_ref, sc_ref, sh_ref, wg_ref, wu_ref, wd_ref, gpost_ref, gate_ref, o_ref, h_ref):
    j = pl.program_id(1)

    tm = x_ref.shape[0]
    slab = min(ROW_SLAB, tm)

    @pl.when(j == 0)
    def _():
        for r in range(tm // slab):
            rows = slice(r * slab, (r + 1) * slab)
            y = _rms(x_ref[rows, :], g_ref[...])
            h_ref[rows, :] = (y * (1.0 + sc_ref[0]) + sh_ref[0]).astype(BF16)
        o_ref[...] = jnp.zeros_like(o_ref)

    h = h_ref[...]
    g = jnp.dot(h, wg_ref[...], preferred_element_type=F32)
    u = jnp.dot(h, wu_ref[...], preferred_element_type=F32)
    a = (g * _sigmoid(g) * u).astype(BF16)
    _accumulate_dot(o_ref, a, wd_ref)

    @pl.when(j == pl.num_programs(1) - 1)
    def _():
        for r in range(tm // slab):
            rows = slice(r * slab, (r + 1) * slab)
            o_ref[rows, :] = x_ref[rows, :] + gate_ref[0] * _rms(o_ref[rows, :], gpost_ref[...])


def _ffn_dense(x2, gain_pre, sc, sh, wg, wu, wd, gain_post, gate, S):
    T, D = x2.shape
    F = wg.shape[1]
    tm = min(TILES["ffn_tm"], S)
    tf = min(TILES["ffn_tf"], F)
    spb = S // tm
    return pl.pallas_call(
        _ffn_kernel,
        out_shape=jax.ShapeDtypeStruct((T, D), F32),
        grid=(T // tm, F // tf),
        in_specs=[pl.BlockSpec((tm, D), lambda i, j: (i, 0)),
                  pl.BlockSpec((1, D), lambda i, j: (0, 0)),
                  pl.BlockSpec((1, 1, D), lambda i, j: (i // spb, 0, 0)),
                  pl.BlockSpec((1, 1, D), lambda i, j: (i // spb, 0, 0)),
                  pl.BlockSpec((D, tf), lambda i, j: (0, j)),
                  pl.BlockSpec((D, tf), lambda i, j: (0, j)),
                  pl.BlockSpec((tf, D), lambda i, j: (j, 0)),
                  pl.BlockSpec((1, D), lambda i, j: (0, 0)),
                  pl.BlockSpec((1, 1, D), lambda i, j: (i // spb, 0, 0))],
        out_specs=pl.BlockSpec((tm, D), lambda i, j: (i, 0)),
        scratch_shapes=[pltpu.VMEM((tm, D), BF16)],
        compiler_params=_cparams(("arbitrary", "arbitrary")),
        name="ffn_dense",
    )(x2, gain_pre.reshape(1, D), sc, sh, wg, wu, wd, gain_post.reshape(1, D), gate)


def _router_kernel(x_ref, g_ref, sc_ref, sh_ref, rw_ref, rb_ref, h_ref, meta_ref, wts_ref, cnt_ref, carry_ref):
    tm = x_ref.shape[0]

    @pl.when(pl.program_id(0) == 0)
    def _():
        carry_ref[...] = jnp.zeros_like(carry_ref)

    h = _rms(x_ref[...], g_ref[...]) * (1.0 + sc_ref[0]) + sh_ref[0]
    h_ref[...] = h
    logits = jnp.dot(h, rw_ref[...], precision=lax.Precision.HIGHEST, preferred_element_type=F32) + rb_ref[...]
    lane = lax.broadcasted_iota(jnp.int32, (tm, LANES), 1)
    m1 = jnp.max(logits, axis=-1, keepdims=True)
    i1 = jnp.min(jnp.where(logits == m1, lane, LANES), axis=-1, keepdims=True)
    oh1 = lane == i1
    rest = jnp.where(oh1, 2.0 * NEG_BIG, logits)
    m2 = jnp.max(rest, axis=-1, keepdims=True)
    i2 = jnp.min(jnp.where(rest == m2, lane, LANES), axis=-1, keepdims=True)
    oh2 = lane == i2
    e2 = jnp.exp(m2 - m1)
    den = 1.0 + e2
    w1 = 1.0 / den
    w2 = e2 / den
    oh = jnp.where(oh1 | oh2, 1.0, 0.0)
    r = lax.broadcasted_iota(jnp.int32, (tm, tm), 0)
    c = lax.broadcasted_iota(jnp.int32, (tm, tm), 1)
    strict = jnp.where(c < r, 1.0, 0.0).astype(BF16)
    carry = carry_ref[0:1, :]
    cum = jnp.dot(strict, oh.astype(BF16), preferred_element_type=F32) + carry
    rank1 = jnp.sum(jnp.where(oh1, cum, 0.0), axis=-1, keepdims=True).astype(jnp.int32)
    rank2 = jnp.sum(jnp.where(oh2, cum, 0.0), axis=-1, keepdims=True).astype(jnp.int32)
    new_carry = carry + jnp.sum(oh, axis=0, keepdims=True)
    carry_ref[...] = jnp.broadcast_to(new_carry, carry_ref.shape)
    cnt_ref[...] = jnp.broadcast_to(new_carry, cnt_ref.shape)
    meta_ref[...] = jnp.where(lane == 0, i1, jnp.where(lane == 1, i2, jnp.where(lane == 2, rank1,
                              jnp.where(lane == 3, rank2, 0))))
    wts_ref[...] = jnp.where(lane == 0, w1, jnp.where(lane == 1, w2, 0.0))


def _router(x2, gain, sc, sh, rw, rb, S):
    T, D = x2.shape
    tm = min(TILES["rt_tm"], S)
    spb = S // tm
    return pl.pallas_call(
        _router_kernel,
        out_shape=(jax.ShapeDtypeStruct((T, D), F32),
                   jax.ShapeDtypeStruct((T, LANES), jnp.int32),
                   jax.ShapeDtypeStruct((T, LANES), F32),
                   jax.ShapeDtypeStruct((SUBLANES, LANES), F32)),
        grid=(T // tm,),
        in_specs=[pl.BlockSpec((tm, D), lambda i: (i, 0)),
                  pl.BlockSpec((1, D), lambda i: (0, 0)),
                  pl.BlockSpec((1, 1, D), lambda i: (i // spb, 0, 0)),
                  pl.BlockSpec((1, 1, D), lambda i: (i // spb, 0, 0)),
                  pl.BlockSpec((D, LANES), lambda i: (0, 0)),
                  pl.BlockSpec((1, LANES), lambda i: (0, 0))],
        out_specs=(pl.BlockSpec((tm, D), lambda i: (i, 0)),
                   pl.BlockSpec((tm, LANES), lambda i: (i, 0)),
                   pl.BlockSpec((tm, LANES), lambda i: (i, 0)),
                   pl.BlockSpec((SUBLANES, LANES), lambda i: (0, 0))),
        scratch_shapes=[pltpu.VMEM((SUBLANES, LANES), F32)],
        compiler_params=_cparams(("arbitrary",)),
        name="moe_router",
    )(x2, gain.reshape(1, D), sc, sh, rw, rb)


ROW_DMA_UNROLL = 8


def _row_copy(src_ref, src_row, dst_ref, dst_row, sem):
    return pltpu.make_async_copy(src_ref.at[pl.ds(src_row, 1)], dst_ref.at[pl.ds(dst_row, 1)], sem)


def _dispatch_kernel(dest_ref, h_ref, xs_in_ref, xs_ref, sem):
    del xs_in_ref
    tm = h_ref.shape[0]

    def issue(r, _):
        for kk in range(TOP_K):
            _row_copy(h_ref, r, xs_ref, dest_ref[0, 0, TOP_K * r + kk], sem).start()
        return 0

    lax.fori_loop(0, tm, issue, 0, unroll=ROW_DMA_UNROLL)
    for kk in range(TOP_K):
        pltpu.make_async_copy(h_ref, xs_ref.at[pl.ds(0, tm)], sem).wait()


def _dispatch(h, dest, n_slots):
    T, D = h.shape
    tm = min(TILES["disp_tm"], T)
    nt = T // tm
    xs0 = jnp.zeros((n_slots, D), F32)
    return pl.pallas_call(
        _dispatch_kernel,
        out_shape=jax.ShapeDtypeStruct((n_slots, D), F32),
        grid=(nt,),
        in_specs=[pl.BlockSpec((1, 1, TOP_K * tm), lambda i: (i, 0, 0), memory_space=pltpu.SMEM),
                  pl.BlockSpec((tm, D), lambda i: (i, 0)),
                  pl.BlockSpec(memory_space=pl.ANY)],
        out_specs=pl.BlockSpec(memory_space=pl.ANY),
        scratch_shapes=[pltpu.SemaphoreType.DMA],
        input_output_aliases={2: 0},
        compiler_params=_cparams(("arbitrary",)),
        name="moe_dispatch",
    )(dest.reshape(nt, 1, TOP_K * tm), h, xs0)


def _expert_kernel(be_ref, nu_ref, xs_ref, wg_ref, wu_ref, wd_ref, o_ref, xb_ref):
    i = pl.program_id(0)
    j = pl.program_id(1)
    active = i < nu_ref[0]

    @pl.when(j == 0)
    def _():
        o_ref[...] = jnp.zeros_like(o_ref)

    @pl.when(active & (j == 0))
    def _():
        xb_ref[...] = xs_ref[...].astype(BF16)

    @pl.when(active)
    def _():
        xb = xb_ref[...]
        g = jnp.dot(xb, wg_ref[0], preferred_element_type=F32)
        u = jnp.dot(xb, wu_ref[0], preferred_element_type=F32)
        a = (g * _sigmoid(g) * u).astype(BF16)
        _accumulate_dot(o_ref, a, wd_ref)


def _experts(xs, block_e, n_used, wg, wu, wd):
    n_slots, D = xs.shape
    E, _, F = wg.shape
    tm = TILES["exp_tm"]
    tf = min(TILES["exp_tf"], F)
    nb = n_slots // tm
    nj = F // tf

    def clamp(i, nu):
        return jnp.minimum(i, nu[0] - 1)

    def x_map(i, j, be, nu):
        return (clamp(i, nu), 0)

    def w_col_map(i, j, be, nu):
        return (be[clamp(i, nu)], 0, jnp.where(i < nu[0], j, nj - 1))

    def w_row_map(i, j, be, nu):
        return (be[clamp(i, nu)], jnp.where(i < nu[0], j, nj - 1), 0)

    return pl.pallas_call(
        _expert_kernel,
        out_shape=jax.ShapeDtypeStruct((n_slots, D), F32),
        grid_spec=pltpu.PrefetchScalarGridSpec(
            num_scalar_prefetch=2,
            grid=(nb, nj),
            in_specs=[pl.BlockSpec((tm, D), x_map, pipeline_mode=pl.Buffered(1)),
                      pl.BlockSpec((1, D, tf), w_col_map),
                      pl.BlockSpec((1, D, tf), w_col_map),
                      pl.BlockSpec((1, tf, D), w_row_map)],
            out_specs=pl.BlockSpec((tm, D), lambda i, j, be, nu: (i, 0)),
            scratch_shapes=[pltpu.VMEM((tm, D), BF16)]),
        compiler_params=_cparams(("arbitrary", "arbitrary")),
        name="moe_experts",
    )(block_e, n_used, xs, wg, wu, wd)


def _combine_kernel(dest_ref, x_ref, wts_ref, gain_ref, gate_ref, ys_ref, o_ref, buf_ref, sem):
    tm = x_ref.shape[0]

    def issue(r, _):
        for kk in range(TOP_K):
            _row_copy(ys_ref, dest_ref[0, 0, TOP_K * r + kk], buf_ref.at[kk], r, sem).start()
        return 0

    lax.fori_loop(0, tm, issue, 0, unroll=ROW_DMA_UNROLL)
    for kk in range(TOP_K):
        pltpu.make_async_copy(ys_ref.at[pl.ds(0, tm)], buf_ref.at[kk], sem).wait()
    w = wts_ref[...]
    y = w[:, 0:1] * buf_ref[0] + w[:, 1:2] * buf_ref[1]
    o_ref[...] = x_ref[...] + gate_ref[0] * _rms(y, gain_ref[...])


def _combine(x2, ys, dest, wts, gain, gate, S):
    T, D = x2.shape
    tm = min(TILES["disp_tm"], S)
    spb = S // tm
    nt = T // tm
    return pl.pallas_call(
        _combine_kernel,
        out_shape=jax.ShapeDtypeStruct((T, D), F32),
        grid=(nt,),
        in_specs=[pl.BlockSpec((1, 1, TOP_K * tm), lambda i: (i, 0, 0), memory_space=pltpu.SMEM),
                  pl.BlockSpec((tm, D), lambda i: (i, 0)),
                  pl.BlockSpec((tm, LANES), lambda i: (i, 0)),
                  pl.BlockSpec((1, D), lambda i: (0, 0)),
                  pl.BlockSpec((1, 1, D), lambda i: (i // spb, 0, 0)),
                  pl.BlockSpec(memory_space=pl.ANY)],
        out_specs=pl.BlockSpec((tm, D), lambda i: (i, 0)),
        scratch_shapes=[pltpu.VMEM((TOP_K, tm, D), F32), pltpu.SemaphoreType.DMA],
        compiler_params=_cparams(("arbitrary",)),
        name="moe_combine",
    )(dest.reshape(nt, 1, TOP_K * tm), x2, wts, gain.reshape(1, D), gate, ys)


def _moe(x2, gain_pre, sc, sh, router_w, router_b, wg, wu, wd, gain_post, gate, S):
    T, D = x2.shape
    E = router_w.shape[1]
    tm_e = TILES["exp_tm"]
    rw = jnp.zeros((D, LANES), F32).at[:, :E].set(router_w)
    rb = jnp.full((1, LANES), NEG_BIG, F32).at[0, :E].set(router_b)
    h, meta, wts, cnt = _router(x2, gain_pre, sc, sh, rw, rb, S)
    counts = cnt[0, :E].astype(jnp.int32)
    padded = ((counts + tm_e - 1) // tm_e) * tm_e
    pend = jnp.cumsum(padded)
    pstart = pend - padded
    dest = pstart[meta[:, 0:TOP_K]] + meta[:, TOP_K:2 * TOP_K]
    n_slots = T * TOP_K + E * tm_e
    nb = n_slots // tm_e
    starts = jnp.arange(nb, dtype=jnp.int32) * tm_e
    block_e = jnp.minimum(jnp.sum((starts[:, None] >= pend[None, :]).astype(jnp.int32), axis=1), E - 1)
    n_used = (pend[-1:] // tm_e).astype(jnp.int32)
    xs = _dispatch(h, dest.reshape(-1), n_slots)
    ys = _experts(xs, block_e, n_used, wg, wu, wd)
    return _combine(x2, ys, dest.reshape(-1), wts, gain_post, gate, S)


def _prep_in_proj(w_in, b_in):
    q_scale = np.float32((MLA_NOPE + MLA_ROPE) ** -0.5 * np.log2(np.e))
    k_scale = np.float32(ML_QK ** -0.5)

    def build(a):
        lead = a.shape[:-1]

        def seg(name):
            s = IN_START[name]
            return a[..., s:s + IN_SIZE[name]]

        mq = seg("mq").reshape(lead + (MLA_HEADS, MLA_NOPE + MLA_ROPE)) * q_scale
        parts = {
            "qn": mq[..., :MLA_NOPE].reshape(lead + (MLA_HEADS * MLA_NOPE,)),
            "qr": mq[..., MLA_NOPE:].reshape(lead + (MLA_HEADS * MLA_ROPE,)),
            "kr": jnp.concatenate([seg("mkr"), jnp.zeros(lead + (LANES - MLA_ROPE,), a.dtype)], axis=-1),
            "gates": jnp.concatenate([seg("li"), seg("lf"),
                                      jnp.zeros(lead + (LANES - 2 * ML_HEADS,), a.dtype)], axis=-1),
            "lk": seg("lk") * k_scale,
            "ckv": seg("mckv"),
            "pad": jnp.zeros(lead + (Z_W["pad"],), a.dtype),
        }
        cols = [parts[n] if n in parts else seg(n) for n, _ in Z_ORDER]
        return jnp.concatenate(cols, axis=-1)

    return build(w_in).astype(BF16), build(b_in)


def kernel(x, c, positions, ada_w, ada_b, norm_pre_mix, norm_post_mix, norm_pre_ffn, norm_post_ffn, w_in, b_in,
           conv_w, sg_norm, sg_w, sg_b, mla_kv_norm, mla_w_uk, mla_w_uv, ml_norm, w_conv_out, w_sg_out,
           w_mla_out, w_ml_out, w_mix_out, ffn_w_gate, ffn_w_up, ffn_w_down, router_w, router_b,
           exp_w_gate, exp_w_up, exp_w_down):
    B, S, D = x.shape
    T = B * S
    x2 = x.reshape(T, D)
    pos = positions.reshape(T, 1).astype(jnp.int32)
    half = MLA_ROPE // 2
    freq32 = ROPE_THETA ** (-jnp.arange(half, dtype=F32) / half)
    freq = jnp.tile(freq32, LANES // half).reshape(1, LANES)
    mod = _ada_mod(c, ada_w, ada_b)
    for l in range(DEPTH):
        sh1, sc1, g1, sh2, sc2, g2 = [m.reshape(B, 1, D) for m in jnp.split(mod[l], 6, axis=-1)]
        w_p, b_p = _prep_in_proj(w_in[l], b_in[l])
        z, zg = _in_proj(x2, norm_pre_mix[l], sc1, sh1, w_p, b_p, S)
        q, k, v = _mla_prep(z, pos, freq, mla_kv_norm[l],
                            mla_w_uk[l].reshape(MLA_KV_RANK, -1).T.astype(BF16),
                            mla_w_uv[l].reshape(MLA_KV_RANK, -1).astype(BF16), B, S)
        y_mla = _attention(q, k, v).reshape(T, MLA_HEADS * MLA_V)
        y_ml = _mlstm(z, zg, ml_norm[l], B, S)
        merged = _merge(z, y_mla, y_ml, conv_w[l], sg_norm[l], sg_w[l], sg_b[l].T,
                        w_conv_out[l].astype(BF16), w_sg_out[l].astype(BF16),
                        w_mla_out[l].astype(BF16), w_ml_out[l].astype(BF16), S)
        x2 = _mix_out(x2, merged, w_mix_out[l].astype(BF16), norm_post_mix[l], g1, S)
        if l % 2 == 0:
            x2 = _ffn_dense(x2, norm_pre_ffn[l], sc2, sh2, ffn_w_gate[l // 2].astype(BF16),
                            ffn_w_up[l // 2].astype(BF16), ffn_w_down[l // 2].astype(BF16),
                            norm_post_ffn[l], g2, S)
        else:
            x2 = _moe(x2, norm_pre_ffn[l], sc2, sh2, router_w[l // 2], router_b[l // 2],
                      exp_w_gate[l // 2].astype(BF16), exp_w_up[l // 2].astype(BF16),
                      exp_w_down[l // 2].astype(BF16), norm_post_ffn[l], g2, S)
    return x2.reshape(B, S, D)
```

```python
import functools

import jax
import jax.numpy as jnp
import numpy as np
from jax import lax
from jax.experimental import pallas as pl
from jax.experimental.pallas import tpu as pltpu

F32 = jnp.float32
BF16 = jnp.bfloat16

D_MODEL = 2048
DEPTH = 2
CHUNK = 64
EPS = 1e-6
CONV_W = 1024
CONV_K = 3
SG_W = 1024
SG_BLOCK = 128
SG_GROUPS = 8
SG_GD = SG_W // SG_GROUPS
MLA_HEADS = 16
MLA_NOPE = 128
MLA_ROPE = 64
MLA_V = 128
MLA_KV_RANK = 512
ROPE_THETA = 10000.0
ML_HEADS = 4
ML_QK = 128
ML_V = 256
D_FF = 5632
N_EXPERTS = 8
TOP_K = 2
D_FF_EXPERT = 7168
IN_SIZES = (
    CONV_W, CONV_W, CONV_W, SG_W, SG_W,
    MLA_HEADS * (MLA_NOPE + MLA_ROPE), MLA_KV_RANK, MLA_ROPE,
    ML_HEADS * ML_QK, ML_HEADS * ML_QK, ML_HEADS * ML_V, ML_HEADS * ML_V, ML_HEADS, ML_HEADS,
    D_MODEL, D_MODEL, D_MODEL, D_MODEL,
)
IN_NAMES = ("cb", "cc", "ch", "su", "sv", "mq", "mckv", "mkr", "lq", "lk", "lv", "lo", "li", "lf",
            "g_conv", "g_sg", "g_mla", "g_ml")
IN_START = {n: sum(IN_SIZES[:j]) for j, n in enumerate(IN_NAMES)}
IN_SIZE = dict(zip(IN_NAMES, IN_SIZES))

LANES = 128
SUBLANES = 8
VMEM_LIMIT = 56 * 1024 * 1024
NEG_BIG = -1e30

Z_ORDER = (("qn", 2048), ("g_conv", 2048), ("g_sg", 2048), ("g_mla", 2048), ("g_ml", 2048),
           ("cb", 1024), ("cc", 1024), ("ch", 1024), ("su", 1024), ("sv", 1024), ("qr", 1024),
           ("lv", 1024), ("lo", 1024), ("ckv", 512), ("lq", 512), ("lk", 512),
           ("kr", 128), ("gates", 128), ("pad", 256))
Z_OFF = {}
_o = 0
for _n, _w in Z_ORDER:
    Z_OFF[_n] = _o
    _o += _w
NZ = _o
Z_W = dict(Z_ORDER)

TILES = dict(
    in_tm=1024, in_tn=2048,
    prep_tm=512,
    attn_tq=1024,
    ml_tm=512, ml_chunk=128,
    m1_tm=256,
    m2_tm=512,
    ffn_tm=512, ffn_tf=512,
    rt_tm=512,
    disp_tm=256,
    exp_tm=512, exp_tf=512,
    ada_tn=1024,
)


def _cparams(sem, vmem=VMEM_LIMIT):
    return pltpu.CompilerParams(dimension_semantics=sem, vmem_limit_bytes=vmem)


def _sigmoid(x):
    return 1.0 / (1.0 + jnp.exp(-x))


def _gelu_tanh(x):
    return 0.5 * x * (1.0 + jnp.tanh(np.float32(np.sqrt(2.0 / np.pi)) * (x + 0.044715 * (x * x * x))))


def _rms(x, gain):
    return x * lax.rsqrt(jnp.mean(x * x, axis=-1, keepdims=True) + EPS) * gain


def _ada_kernel(c_ref, w_ref, b_ref, o_ref):
    c = c_ref[...]
    cs = (c * _sigmoid(c)).astype(BF16)
    o_ref[0] = jnp.dot(cs, w_ref[0].astype(BF16), preferred_element_type=F32) + b_ref[0]


def _ada_mod(c, ada_w, ada_b):
    L, D, N = ada_w.shape
    B = c.shape[0]
    tn = min(TILES["ada_tn"], N)
    c_pad = jnp.zeros((SUBLANES, D), F32).at[:B].set(c)
    out = pl.pallas_call(
        _ada_kernel,
        out_shape=jax.ShapeDtypeStruct((L, SUBLANES, N), F32),
        grid=(L, N // tn),
        in_specs=[pl.BlockSpec((SUBLANES, D), lambda l, j: (0, 0)),
                  pl.BlockSpec((1, D, tn), lambda l, j: (l, 0, j)),
                  pl.BlockSpec((1, 1, tn), lambda l, j: (l, 0, j))],
        out_specs=pl.BlockSpec((1, SUBLANES, tn), lambda l, j: (l, 0, j)),
        compiler_params=_cparams(("arbitrary", "arbitrary")),
        name="ada_mod",
    )(c_pad, ada_w, ada_b.reshape(L, 1, N))
    return out[:, :B]


def _in_proj_kernel(x_ref, g_ref, sc_ref, sh_ref, w_ref, b_ref, o_ref, og_ref, h_ref, *, gates_tile, gates_off):
    tm, tn = o_ref.shape
    slab = min(ROW_SLAB, tm)

    @pl.when(pl.program_id(1) == 0)
    def _():
        for r in range(tm // slab):
            rows = slice(r * slab, (r + 1) * slab)
            y = _rms(x_ref[rows, :], g_ref[...])
            h_ref[rows, :] = (y * (1.0 + sc_ref[0]) + sh_ref[0]).astype(BF16)

    h = h_ref[...]
    step = min(DOWN_PROJ_COLS, tn)
    for c in range(tn // step):
        cols = slice(c * step, (c + 1) * step)
        acc = jnp.dot(h, w_ref[:, cols], preferred_element_type=F32) + b_ref[:, cols]
        o_ref[:, cols] = acc.astype(o_ref.dtype)
        if c == gates_off // step:
            @pl.when(pl.program_id(1) == gates_tile)
            def _(acc=acc):
                og_ref[...] = acc[:, gates_off % step:gates_off % step + LANES]


def _in_proj(x2, gain, sc, sh, w, b, S):
    T, D = x2.shape
    N = w.shape[1]
    tm = min(TILES["in_tm"], S)
    tn = min(TILES["in_tn"], N)
    spb = S // tm
    kern = functools.partial(_in_proj_kernel, gates_tile=Z_OFF["gates"] // tn, gates_off=Z_OFF["gates"] % tn)
    return pl.pallas_call(
        kern,
        out_shape=(jax.ShapeDtypeStruct((T, N), BF16), jax.ShapeDtypeStruct((T, LANES), F32)),
        grid=(T // tm, N // tn),
        in_specs=[pl.BlockSpec((tm, D), lambda i, j: (i, 0)),
                  pl.BlockSpec((1, D), lambda i, j: (0, 0)),
                  pl.BlockSpec((1, 1, D), lambda i, j: (i // spb, 0, 0)),
                  pl.BlockSpec((1, 1, D), lambda i, j: (i // spb, 0, 0)),
                  pl.BlockSpec((D, tn), lambda i, j: (0, j)),
                  pl.BlockSpec((1, tn), lambda i, j: (0, j))],
        out_specs=(pl.BlockSpec((tm, tn), lambda i, j: (i, j)),
                   pl.BlockSpec((tm, LANES), lambda i, j: (i, 0))),
        scratch_shapes=[pltpu.VMEM((tm, D), BF16)],
        compiler_params=_cparams(("arbitrary", "arbitrary")),
        name="in_proj",
    )(x2, gain.reshape(1, D), sc, sh, w, b.reshape(1, N))


def _swap_halves(v, first_half):
    return jnp.where(first_half, pltpu.roll(v, LANES - MLA_ROPE // 2, 1), pltpu.roll(v, MLA_ROPE // 2, 1))


def _mla_prep_kernel(qn_ref, qr_ref, ckv_ref, kr_ref, pos_ref, freq_ref, kvn_ref, wuk_ref, wuv_ref,
                     q_ref, k_ref, v_ref):
    tm = qn_ref.shape[0]
    ang = pos_ref[...].astype(F32) * freq_ref[...]
    cos = jnp.cos(ang)
    sin = jnp.sin(ang)
    lane = lax.broadcasted_iota(jnp.int32, (1, LANES), 1)
    first_half = (lane % MLA_ROPE) < (MLA_ROPE // 2)
    sgn_sin = jnp.where(first_half, -sin, sin)

    for c in range(MLA_HEADS // 2):
        v = qr_ref[:, c * LANES:(c + 1) * LANES].astype(F32)
        r = (v * cos + _swap_halves(v, first_half) * sgn_sin).astype(BF16)
        for h in (2 * c, 2 * c + 1):
            q_ref[0, h, :, 0:MLA_NOPE] = qn_ref[:, h * MLA_NOPE:(h + 1) * MLA_NOPE]
            q_ref[0, h, :, MLA_NOPE:2 * MLA_NOPE] = r

    cn = _rms(ckv_ref[...].astype(F32), kvn_ref[...]).astype(BF16)
    knt = lax.dot_general(wuk_ref[...], cn, (((1,), (1,)), ((), ())), preferred_element_type=F32).astype(BF16)
    vv = jnp.dot(cn, wuv_ref[...], preferred_element_type=F32).astype(BF16)
    ka = kr_ref[...].astype(F32)
    kr_even = ka * cos + _swap_halves(ka, first_half) * sgn_sin
    kr_odd = pltpu.roll(kr_even, MLA_ROPE, 1)
    krt_even = kr_even.T.astype(BF16)
    krt_odd = kr_odd.T.astype(BF16)
    ones_col = jnp.where(lax.broadcasted_iota(jnp.int32, (tm, LANES), 1) == 0, 1.0, 0.0).astype(BF16)
    for h in range(MLA_HEADS):
        k_ref[0, h, 0:MLA_NOPE, :] = knt[h * MLA_NOPE:(h + 1) * MLA_NOPE, :]
        k_ref[0, h, MLA_NOPE:2 * MLA_NOPE, :] = krt_even if h % 2 == 0 else krt_odd
        v_ref[0, h, :, 0:MLA_V] = vv[:, h * MLA_V:(h + 1) * MLA_V]
        v_ref[0, h, :, MLA_V:MLA_V + LANES] = ones_col


def _mla_prep(z, pos, freq, kv_norm, w_uk_t, w_uv, B, S):
    tm = min(TILES["prep_tm"], S)
    spb = S // tm
    H = MLA_HEADS
    KD = 2 * MLA_NOPE

    def zspec(name):
        w = Z_W[name]
        cb = Z_OFF[name] // w
        return pl.BlockSpec((tm, w), lambda b, i: (b * spb + i, cb))

    return pl.pallas_call(
        _mla_prep_kernel,
        out_shape=(jax.ShapeDtypeStruct((B, H, S, KD), BF16),
                   jax.ShapeDtypeStruct((B, H, KD, S), BF16),
                   jax.ShapeDtypeStruct((B, H, S, MLA_V + LANES), BF16)),
        grid=(B, spb),
        in_specs=[zspec("qn"), zspec("qr"), zspec("ckv"), zspec("kr"),
                  pl.BlockSpec((tm, 1), lambda b, i: (b * spb + i, 0)),
                  pl.BlockSpec((1, LANES), lambda b, i: (0, 0)),
                  pl.BlockSpec((1, MLA_KV_RANK), lambda b, i: (0, 0)),
                  pl.BlockSpec((H * MLA_NOPE, MLA_KV_RANK), lambda b, i: (0, 0)),
                  pl.BlockSpec((MLA_KV_RANK, H * MLA_V), lambda b, i: (0, 0))],
        out_specs=(pl.BlockSpec((1, H, tm, KD), lambda b, i: (b, 0, i, 0)),
                   pl.BlockSpec((1, H, KD, tm), lambda b, i: (b, 0, 0, i)),
                   pl.BlockSpec((1, H, tm, MLA_V + LANES), lambda b, i: (b, 0, i, 0))),
        compiler_params=_cparams(("arbitrary", "arbitrary")),
        name="mla_prep",
    )(z, z, z, z, pos, freq, kv_norm.reshape(1, -1), w_uk_t, w_uv)


ATTN_GROUPS = 2
ATTN_STRIP = 32
ATTN_KTILE = 256
ATTN_PAIRS_PER_TRIP = 2


def _attn_kernel(q_ref, kt_ref, v_ref, o_ref, s0_ref, s1_ref, *scratch):
    G = ATTN_GROUPS
    p_refs, acc_refs, m_refs, bm_refs, al_refs = (scratch[n * G:(n + 1) * G] for n in range(5))
    tq = q_ref.shape[2]
    tk = tq // G
    va = v_ref.shape[3]
    i = pl.program_id(2)
    for g in range(G):
        m_refs[g][...] = jnp.full(m_refs[g].shape, NEG_BIG, F32)
        acc_refs[g][...] = jnp.zeros(acc_refs[g].shape, F32)

    def scores(blk, s_ref, first_row=0):
        start = pl.multiple_of(blk * tk, tk)
        s_ref[first_row:, :] = jnp.dot(q_ref[0, 0, first_row:, :], kt_ref[0, 0, :, pl.ds(start, tk)],
                                       preferred_element_type=F32)

    def softmax_pv(blk, s_ref, modes):
        v = v_ref[0, 0, pl.ds(pl.multiple_of(blk * tk, tk), tk), :]
        for g in range(G):
            if modes[g] == "skip":
                continue
            p_ref, acc_ref, m_ref, bm_ref, al_ref = p_refs[g], acc_refs[g], m_refs[g], bm_refs[g], al_refs[g]

            def load_strip(r, cols=slice(0, tk)):
                x = s_ref[g * tk + r * ATTN_STRIP:g * tk + (r + 1) * ATTN_STRIP, cols]
                if modes[g] == "diag":
                    row = (lax.broadcasted_iota(jnp.int32, x.shape, 0) + r * ATTN_STRIP) // CHUNK
                    col = (lax.broadcasted_iota(jnp.int32, x.shape, 1) + cols.start) // CHUNK
                    x = jnp.where(col <= row, x, NEG_BIG)
                return x

            for r in range(tk // ATTN_STRIP):
                rows = slice(r * ATTN_STRIP, (r + 1) * ATTN_STRIP)
                bm_ref[rows, :] = jnp.broadcast_to(jnp.max(load_strip(r), axis=-1, keepdims=True),
                                                   (ATTN_STRIP, LANES))
            m_old = m_ref[...]
            m_new = jnp.maximum(m_old, bm_ref[...])
            m_ref[...] = m_new
            al_ref[...] = jnp.exp2(m_old - m_new)
            pv = acc_ref[...] * jnp.concatenate([al_ref[...]] * (va // LANES), axis=1)
            kt = min(ATTN_KTILE, tk)
            for c in range(tk // kt):
                cols = slice(c * kt, (c + 1) * kt)
                for r in range(tk // ATTN_STRIP):
                    rows = slice(r * ATTN_STRIP, (r + 1) * ATTN_STRIP)
                    p = jnp.exp2(load_strip(r, cols) - jnp.concatenate([m_ref[rows, :]] * (kt // LANES), axis=1))
                    p_ref[rows, cols] = p.astype(BF16)
                pv = pv + jnp.dot(p_ref[:, cols], v[cols, :], preferred_element_type=F32)
            acc_ref[...] = pv

    scores(0, s0_ref)

    def visible_pair(t):
        scores(2 * t + 1, s1_ref)
        softmax_pv(2 * t, s0_ref, ("full", "full"))
        scores(2 * t + 2, s0_ref)
        softmax_pv(2 * t + 1, s1_ref, ("full", "full"))

    def body(u, carry):
        for n in range(ATTN_PAIRS_PER_TRIP):
            visible_pair(ATTN_PAIRS_PER_TRIP * u + n)
        return carry

    lax.fori_loop(0, i // ATTN_PAIRS_PER_TRIP, body, 0)
    for n in range(1, ATTN_PAIRS_PER_TRIP):
        @pl.when(i % ATTN_PAIRS_PER_TRIP >= n)
        def _(n=n):
            visible_pair(i - i % ATTN_PAIRS_PER_TRIP + n - 1)

    scores(2 * i + 1, s1_ref, first_row=tk)
    softmax_pv(2 * i, s0_ref, ("diag", "full"))
    softmax_pv(2 * i + 1, s1_ref, ("skip", "diag"))

    for g in range(G):
        acc = acc_refs[g][...]
        o_ref[0, g * tk:(g + 1) * tk, :] = (acc[:, :MLA_V] / acc[:, MLA_V:MLA_V + 1]).astype(o_ref.dtype)


def _attention(q, kt, v):
    B, H, S, KD = q.shape
    VA = v.shape[3]
    tq = min(TILES["attn_tq"], S)
    hq = tq // ATTN_GROUPS
    assert hq % CHUNK == 0 and hq % ATTN_STRIP == 0 and S % tq == 0
    return pl.pallas_call(
        _attn_kernel,
        out_shape=jax.ShapeDtypeStruct((B, S, H * MLA_V), BF16),
        grid=(B, H, S // tq),
        in_specs=[pl.BlockSpec((1, 1, tq, KD), lambda b, h, i: (b, h, i, 0)),
                  pl.BlockSpec((1, 1, KD, S), lambda b, h, i: (b, h, 0, 0)),
                  pl.BlockSpec((1, 1, S, VA), lambda b, h, i: (b, h, 0, 0))],
        out_specs=pl.BlockSpec((1, tq, MLA_V), lambda b, h, i: (b, i, h)),
        scratch_shapes=([pltpu.VMEM((tq, hq), F32)] * 2 + [pltpu.VMEM((hq, hq), BF16)] * ATTN_GROUPS
                        + [pltpu.VMEM((hq, VA), F32)] * ATTN_GROUPS
                        + [pltpu.VMEM((hq, LANES), F32)] * (3 * ATTN_GROUPS)),
        compiler_params=_cparams(("arbitrary", "arbitrary", "arbitrary")),
        name="mla_attention",
    )(q, kt, v)


ML_VA = ML_V + LANES


def _log_sigmoid(x):
    return -(jnp.maximum(-x, 0.0) + jnp.log1p(jnp.exp(-jnp.abs(x))))


def _mlstm_kernel(q_ref, k_ref, v_ref, o_ref, g_ref, gain_ref, y_ref, *state):
    L = TILES["ml_chunk"]
    nb, tm = q_ref.shape[0], q_ref.shape[1]
    assert L == LANES and tm % L == 0
    st_refs, m_refs = state[:nb * ML_HEADS], state[nb * ML_HEADS:]

    @pl.when(pl.program_id(0) == 0)
    def _():
        for r in st_refs + m_refs:
            r[...] = jnp.zeros_like(r)

    row = lax.broadcasted_iota(jnp.int32, (L, L), 0)
    col = lax.broadcasted_iota(jnp.int32, (L, L), 1)
    causal = col <= row
    tril = jnp.where(causal, 1.0, 0.0).astype(F32)
    triu = jnp.where(row <= col, 1.0, 0.0).astype(F32)
    lane_a = lax.broadcasted_iota(jnp.int32, (L, LANES), 1)
    ones_col = jnp.where(lane_a == 0, 1.0, 0.0).astype(BF16)

    def chunk(c, _):
        r0 = pl.multiple_of(c * L, L)
        for b in range(nb):
            chunk_one_batch(b, r0)
        return 0

    def chunk_one_batch(b, r0):
        gf = g_ref[b, pl.ds(r0, L), :]
        gt = gf.T
        lf_c = _log_sigmoid(gf)
        b_c = jnp.dot(tril, lf_c, precision=lax.Precision.HIGHEST, preferred_element_type=F32)
        lf_r = _log_sigmoid(gt)
        b_r = jnp.dot(lf_r, triu, precision=lax.Precision.HIGHEST, preferred_element_type=F32)
        for h in range(ML_HEADS):
            sh = b * ML_HEADS + h
            q = q_ref[b, pl.ds(r0, L), h * ML_QK:(h + 1) * ML_QK]
            k = k_ref[b, pl.ds(r0, L), h * ML_QK:(h + 1) * ML_QK]
            v = v_ref[b, pl.ds(r0, L), h * ML_V:(h + 1) * ML_V]
            vaug = jnp.concatenate([v, ones_col], axis=1)
            bcol = b_c[:, ML_HEADS + h:ML_HEADS + h + 1]
            icol = gf[:, h:h + 1]
            brow = b_r[ML_HEADS + h:ML_HEADS + h + 1, :]
            irow = gt[h:h + 1, :]
            m0 = m_refs[sh][0:1, 0:1]
            dl = jnp.where(causal, bcol - brow + irow, NEG_BIG)
            inter_log = bcol + m0
            m_t = jnp.maximum(inter_log, jnp.max(dl, axis=-1, keepdims=True))
            dm = jnp.exp(dl - m_t)
            s = lax.dot_general(q, k, (((1,), (1,)), ((), ())), preferred_element_type=F32)
            p = (s * dm).astype(BF16)
            st = st_refs[sh][...]
            intra = jnp.dot(p, vaug, preferred_element_type=F32)
            inter = jnp.dot(q, st.astype(BF16), preferred_element_type=F32)
            tot = intra + jnp.exp(inter_log - m_t) * inter
            num = tot[:, :ML_V]
            den = tot[:, ML_V:ML_V + 1]
            hv = num / jnp.maximum(jnp.abs(den), jnp.exp(-m_t))
            hn = _rms(hv, gain_ref[:, h * ML_V:(h + 1) * ML_V])
            og = o_ref[b, pl.ds(r0, L), h * ML_V:(h + 1) * ML_V].astype(F32)
            y_ref[b, pl.ds(r0, L), h * ML_V:(h + 1) * ML_V] = (_sigmoid(og) * hn).astype(y_ref.dtype)
            b_end = bcol[L - 1:L, :]
            a = b_end - bcol + icol
            g = jnp.max(a, axis=0, keepdims=True)
            wa = jnp.exp(a - g)
            kt = k.astype(F32).T.astype(BF16)
            upd = jnp.dot(kt, (vaug.astype(F32) * wa).astype(BF16), preferred_element_type=F32)
            m_new = jnp.maximum(b_end + m0, g)
            decay = jnp.exp(b_end + m0 - m_new)
            inject = jnp.exp(g - m_new)
            st_refs[sh][...] = decay * st + inject * upd
            m_refs[sh][...] = jnp.broadcast_to(m_new, (SUBLANES, LANES))

    lax.fori_loop(0, tm // L, chunk, 0)


def _mlstm(z, zg, ml_norm, B, S):
    tm = min(TILES["ml_tm"], S)
    T = B * S
    z3 = z.reshape(B, S, z.shape[1])

    def zspec(name):
        w = Z_W[name]
        cb = Z_OFF[name] // w
        return pl.BlockSpec((B, tm, w), lambda i: (0, i, cb))

    y = pl.pallas_call(
        _mlstm_kernel,
        out_shape=jax.ShapeDtypeStruct((B, S, ML_HEADS * ML_V), BF16),
        grid=(S // tm,),
        in_specs=[zspec("lq"), zspec("lk"), zspec("lv"), zspec("lo"),
                  pl.BlockSpec((B, tm, LANES), lambda i: (0, i, 0)),
                  pl.BlockSpec((1, ML_HEADS * ML_V), lambda i: (0, 0))],
        out_specs=pl.BlockSpec((B, tm, ML_HEADS * ML_V), lambda i: (0, i, 0)),
        scratch_shapes=([pltpu.VMEM((ML_QK, ML_VA), F32)] * (B * ML_HEADS)
                        + [pltpu.VMEM((SUBLANES, LANES), F32)] * (B * ML_HEADS)),
        compiler_params=_cparams(("arbitrary",)),
        name="mlstm",
    )(z3, z3, z3, z3, zg.reshape(B, S, LANES), ml_norm.reshape(1, -1))
    return y.reshape(T, ML_HEADS * ML_V)


def _merge_kernel(cb_ref, cc_ref, ch_ref, ccp_ref, chp_ref, su_ref, sv_ref,
                  gc_ref, gs_ref, gm_ref, gl_ref, ymla_ref, yml_ref,
                  convw_ref, sgn_ref, sgw_ref, sgb_ref, wc_ref, ws_ref, wm_ref, wl_ref,
                  o_ref, *, steps_per_batch):
    tm = cb_ref.shape[0]
    i = pl.program_id(0)
    z = cc_ref[...].astype(F32) * ch_ref[...].astype(F32)
    zp = ccp_ref[...].astype(F32) * chp_ref[...].astype(F32)
    zp = jnp.where(i % steps_per_batch == 0, 0.0, zp)
    rid = lax.broadcasted_iota(jnp.int32, (tm, 1), 0)
    z1 = jnp.where(rid == 0, zp[SUBLANES - 1:SUBLANES, :], pltpu.roll(z, 1, 0))
    z2 = jnp.where(rid == 0, zp[SUBLANES - 2:SUBLANES - 1, :],
                   jnp.where(rid == 1, zp[SUBLANES - 1:SUBLANES, :], pltpu.roll(z, 2, 0)))
    cw = convw_ref[...]
    y_conv = cb_ref[...].astype(F32) * (cw[0:1, :] * z2 + cw[1:2, :] * z1 + cw[2:3, :] * z)
    acc = _sigmoid(gc_ref[...].astype(F32)) * jnp.dot(y_conv.astype(BF16), wc_ref[...],
                                                      preferred_element_type=F32)
    u = _gelu_tanh(su_ref[...].astype(F32))
    vn = _rms(_gelu_tanh(sv_ref[...].astype(F32)), sgn_ref[...]).astype(BF16)
    r = lax.broadcasted_iota(jnp.int32, (SG_BLOCK, SG_BLOCK), 0)
    c = lax.broadcasted_iota(jnp.int32, (SG_BLOCK, SG_BLOCK), 1)
    blocks = []
    for n in range(tm // SG_BLOCK):
        cols = []
        for g in range(SG_GROUPS):
            w = jnp.where(c <= r, sgw_ref[g], 0.0).astype(BF16)
            vb = vn[n * SG_BLOCK:(n + 1) * SG_BLOCK, g * SG_GD:(g + 1) * SG_GD]
            cols.append(jnp.dot(w, vb, preferred_element_type=F32) + sgb_ref[:, g:g + 1])
        blocks.append(jnp.concatenate(cols, axis=1))
    mixed = blocks[0] if len(blocks) == 1 else jnp.concatenate(blocks, axis=0)
    y_sg = (u * mixed).astype(BF16)
    acc += _sigmoid(gs_ref[...].astype(F32)) * jnp.dot(y_sg, ws_ref[...], preferred_element_type=F32)
    acc += _sigmoid(gm_ref[...].astype(F32)) * jnp.dot(ymla_ref[...], wm_ref[...], preferred_element_type=F32)
    acc += _sigmoid(gl_ref[...].astype(F32)) * jnp.dot(yml_ref[...], wl_ref[...], preferred_element_type=F32)
    o_ref[...] = acc.astype(o_ref.dtype)


def _merge(z, y_mla, y_ml, conv_w, sg_norm, sg_w, sg_bt, wc, ws, wm, wl, S):
    T = z.shape[0]
    D = wc.shape[1]
    tm = min(TILES["m1_tm"], S)
    spb = S // tm
    rows8 = tm // SUBLANES

    def zspec(name):
        w = Z_W[name]
        cb = Z_OFF[name] // w
        return pl.BlockSpec((tm, w), lambda i: (i, cb))

    def zprev(name):
        w = Z_W[name]
        cb = Z_OFF[name] // w
        return pl.BlockSpec((SUBLANES, w), lambda i: (jnp.maximum(i * rows8 - 1, 0), cb))

    def const(shape):
        nd = len(shape)
        return pl.BlockSpec(shape, lambda i: (0,) * nd, pipeline_mode=pl.Buffered(1))

    return pl.pallas_call(
        functools.partial(_merge_kernel, steps_per_batch=spb),
        out_shape=jax.ShapeDtypeStruct((T, D), BF16),
        grid=(T // tm,),
        in_specs=[zspec("cb"), zspec("cc"), zspec("ch"), zprev("cc"), zprev("ch"), zspec("su"), zspec("sv"),
                  zspec("g_conv"), zspec("g_sg"), zspec("g_mla"), zspec("g_ml"),
                  pl.BlockSpec((tm, y_mla.shape[1]), lambda i: (i, 0)),
                  pl.BlockSpec((tm, y_ml.shape[1]), lambda i: (i, 0)),
                  const(conv_w.shape), const((1, SG_W)), const(sg_w.shape), const(sg_bt.shape),
                  const(wc.shape), const(ws.shape), const(wm.shape), const(wl.shape)],
        out_specs=pl.BlockSpec((tm, D), lambda i: (i, 0)),
        compiler_params=_cparams(("arbitrary",)),
        name="merge",
    )(z, z, z, z, z, z, z, z, z, z, z, y_mla, y_ml,
      conv_w, sg_norm.reshape(1, -1), sg_w, sg_bt, wc, ws, wm, wl)


def _mix_out_kernel(x_ref, m_ref, w_ref, gain_ref, gate_ref, o_ref):
    y = jnp.dot(m_ref[...], w_ref[...], preferred_element_type=F32)
    o_ref[...] = x_ref[...] + gate_ref[0] * _rms(y, gain_ref[...])


def _mix_out(x2, merged, w, gain, gate, S):
    T, D = x2.shape
    tm = min(TILES["m2_tm"], S)
    spb = S // tm
    return pl.pallas_call(
        _mix_out_kernel,
        out_shape=jax.ShapeDtypeStruct((T, D), F32),
        grid=(T // tm,),
        in_specs=[pl.BlockSpec((tm, D), lambda i: (i, 0)),
                  pl.BlockSpec((tm, D), lambda i: (i, 0)),
                  pl.BlockSpec((D, D), lambda i: (0, 0), pipeline_mode=pl.Buffered(1)),
                  pl.BlockSpec((1, D), lambda i: (0, 0)),
                  pl.BlockSpec((1, 1, D), lambda i: (i // spb, 0, 0))],
        out_specs=pl.BlockSpec((tm, D), lambda i: (i, 0)),
        compiler_params=_cparams(("arbitrary",)),
        name="mix_out",
    )(x2, merged, w, gain.reshape(1, D), gate)


DOWN_PROJ_COLS = 512
ROW_SLAB = 256


def _accumulate_dot(o_ref, a, w_ref):
    n = o_ref.shape[1]
    step = min(DOWN_PROJ_COLS, n)
    for c in range(n // step):
        cols = slice(c * step, (c + 1) * step)
        w = w_ref[:, cols] if len(w_ref.shape) == 2 else w_ref[0, :, cols]
        o_ref[:, cols] += jnp.dot(a, w, preferred_element_type=F32)


def _ffn_kernel(x_ref, g_ref, sc_ref, sh_ref, wg_ref, wu_ref, wd_ref, gpost_ref, gate_ref, o_ref, h_ref):
    j = pl.program_id(1)

    tm = x_ref.shape[0]
    slab = min(ROW_SLAB, tm)

    @pl.when(j == 0)
    def _():
        for r in range(tm // slab):
            rows = slice(r * slab, (r + 1) * slab)
            y = _rms(x_ref[rows, :], g_ref[...])
            h_ref[rows, :] = (y * (1.0 + sc_ref[0]) + sh_ref[0]).astype(BF16)
        o_ref[...] = jnp.zeros_like(o_ref)

    h = h_ref[...]
    g = jnp.dot(h, wg_ref[...], preferred_element_type=F32)
    u = jnp.dot(h, wu_ref[...], preferred_element_type=F32)
    a = (g * _sigmoid(g) * u).astype(BF16)
    _accumulate_dot(o_ref, a, wd_ref)

    @pl.when(j == pl.num_programs(1) - 1)
    def _():
        for r in range(tm // slab):
            rows = slice(r * slab, (r + 1) * slab)
            o_ref[rows, :] = x_ref[rows, :] + gate_ref[0] * _rms(o_ref[rows, :], gpost_ref[...])


def _ffn_dense(x2, gain_pre, sc, sh, wg, wu, wd, gain_post, gate, S):
    T, D = x2.shape
    F = wg.shape[1]
    tm = min(TILES["ffn_tm"], S)
    tf = min(TILES["ffn_tf"], F)
    spb = S // tm
    return pl.pallas_call(
        _ffn_kernel,
        out_shape=jax.ShapeDtypeStruct((T, D), F32),
        grid=(T // tm, F // tf),
        in_specs=[pl.BlockSpec((tm, D), lambda i, j: (i, 0)),
                  pl.BlockSpec((1, D), lambda i, j: (0, 0)),
                  pl.BlockSpec((1, 1, D), lambda i, j: (i // spb, 0, 0)),
                  pl.BlockSpec((1, 1, D), lambda i, j: (i // spb, 0, 0)),
                  pl.BlockSpec((D, tf), lambda i, j: (0, j)),
                  pl.BlockSpec((D, tf), lambda i, j: (0, j)),
                  pl.BlockSpec((tf, D), lambda i, j: (j, 0)),
                  pl.BlockSpec((1, D), lambda i, j: (0, 0)),
                  pl.BlockSpec((1, 1, D), lambda i, j: (i // spb, 0, 0))],
        out_specs=pl.BlockSpec((tm, D), lambda i, j: (i, 0)),
        scratch_shapes=[pltpu.VMEM((tm, D), BF16)],
        compiler_params=_cparams(("arbitrary", "arbitrary")),
        name="ffn_dense",
    )(x2, gain_pre.reshape(1, D), sc, sh, wg, wu, wd, gain_post.reshape(1, D), gate)


def _router_kernel(x_ref, g_ref, sc_ref, sh_ref, rw_ref, rb_ref, h_ref, meta_ref, wts_ref, cnt_ref, carry_ref):
    tm = x_ref.shape[0]

    @pl.when(pl.program_id(0) == 0)
    def _():
        carry_ref[...] = jnp.zeros_like(carry_ref)

    h = _rms(x_ref[...], g_ref[...]) * (1.0 + sc_ref[0]) + sh_ref[0]
    h_ref[...] = h
    logits = jnp.dot(h, rw_ref[...], precision=lax.Precision.HIGHEST, preferred_element_type=F32) + rb_ref[...]
    lane = lax.broadcasted_iota(jnp.int32, (tm, LANES), 1)
    m1 = jnp.max(logits, axis=-1, keepdims=True)
    i1 = jnp.min(jnp.where(logits == m1, lane, LANES), axis=-1, keepdims=True)
    oh1 = lane == i1
    rest = jnp.where(oh1, 2.0 * NEG_BIG, logits)
    m2 = jnp.max(rest, axis=-1, keepdims=True)
    i2 = jnp.min(jnp.where(rest == m2, lane, LANES), axis=-1, keepdims=True)
    oh2 = lane == i2
    e2 = jnp.exp(m2 - m1)
    den = 1.0 + e2
    w1 = 1.0 / den
    w2 = e2 / den
    oh = jnp.where(oh1 | oh2, 1.0, 0.0)
    r = lax.broadcasted_iota(jnp.int32, (tm, tm), 0)
    c = lax.broadcasted_iota(jnp.int32, (tm, tm), 1)
    strict = jnp.where(c < r, 1.0, 0.0).astype(BF16)
    carry = carry_ref[0:1, :]
    cum = jnp.dot(strict, oh.astype(BF16), preferred_element_type=F32) + carry
    rank1 = jnp.sum(jnp.where(oh1, cum, 0.0), axis=-1, keepdims=True).astype(jnp.int32)
    rank2 = jnp.sum(jnp.where(oh2, cum, 0.0), axis=-1, keepdims=True).astype(jnp.int32)
    new_carry = carry + jnp.sum(oh, axis=0, keepdims=True)
    carry_ref[...] = jnp.broadcast_to(new_carry, carry_ref.shape)
    cnt_ref[...] = jnp.broadcast_to(new_carry, cnt_ref.shape)
    meta_ref[...] = jnp.where(lane == 0, i1, jnp.where(lane == 1, i2, jnp.where(lane == 2, rank1,
                              jnp.where(lane == 3, rank2, 0))))
    wts_ref[...] = jnp.where(lane == 0, w1, jnp.where(lane == 1, w2, 0.0))


def _router(x2, gain, sc, sh, rw, rb, S):
    T, D = x2.shape
    tm = min(TILES["rt_tm"], S)
    spb = S // tm
    return pl.pallas_call(
        _router_kernel,
        out_shape=(jax.ShapeDtypeStruct((T, D), F32),
                   jax.ShapeDtypeStruct((T, LANES), jnp.int32),
                   jax.ShapeDtypeStruct((T, LANES), F32),
                   jax.ShapeDtypeStruct((SUBLANES, LANES), F32)),
        grid=(T // tm,),
        in_specs=[pl.BlockSpec((tm, D), lambda i: (i, 0)),
                  pl.BlockSpec((1, D), lambda i: (0, 0)),
                  pl.BlockSpec((1, 1, D), lambda i: (i // spb, 0, 0)),
                  pl.BlockSpec((1, 1, D), lambda i: (i // spb, 0, 0)),
                  pl.BlockSpec((D, LANES), lambda i: (0, 0)),
                  pl.BlockSpec((1, LANES), lambda i: (0, 0))],
        out_specs=(pl.BlockSpec((tm, D), lambda i: (i, 0)),
                   pl.BlockSpec((tm, LANES), lambda i: (i, 0)),
                   pl.BlockSpec((tm, LANES), lambda i: (i, 0)),
                   pl.BlockSpec((SUBLANES, LANES), lambda i: (0, 0))),
        scratch_shapes=[pltpu.VMEM((SUBLANES, LANES), F32)],
        compiler_params=_cparams(("arbitrary",)),
        name="moe_router",
    )(x2, gain.reshape(1, D), sc, sh, rw, rb)


ROW_DMA_UNROLL = 8


def _row_copy(src_ref, src_row, dst_ref, dst_row, sem):
    return pltpu.make_async_copy(src_ref.at[pl.ds(src_row, 1)], dst_ref.at[pl.ds(dst_row, 1)], sem)


def _dispatch_kernel(dest_ref, h_ref, xs_in_ref, xs_ref, sem):
    del xs_in_ref
    tm = h_ref.shape[0]

    def issue(r, _):
        for kk in range(TOP_K):
            _row_copy(h_ref, r, xs_ref, dest_ref[0, 0, TOP_K * r + kk], sem).start()
        return 0

    lax.fori_loop(0, tm, issue, 0, unroll=ROW_DMA_UNROLL)
    for kk in range(TOP_K):
        pltpu.make_async_copy(h_ref, xs_ref.at[pl.ds(0, tm)], sem).wait()


def _dispatch(h, dest, n_slots):
    T, D = h.shape
    tm = min(TILES["disp_tm"], T)
    nt = T // tm
    xs0 = jnp.zeros((n_slots, D), F32)
    return pl.pallas_call(
        _dispatch_kernel,
        out_shape=jax.ShapeDtypeStruct((n_slots, D), F32),
        grid=(nt,),
        in_specs=[pl.BlockSpec((1, 1, TOP_K * tm), lambda i: (i, 0, 0), memory_space=pltpu.SMEM),
                  pl.BlockSpec((tm, D), lambda i: (i, 0)),
                  pl.BlockSpec(memory_space=pl.ANY)],
        out_specs=pl.BlockSpec(memory_space=pl.ANY),
        scratch_shapes=[pltpu.SemaphoreType.DMA],
        input_output_aliases={2: 0},
        compiler_params=_cparams(("arbitrary",)),
        name="moe_dispatch",
    )(dest.reshape(nt, 1, TOP_K * tm), h, xs0)


def _expert_kernel(be_ref, nu_ref, xs_ref, wg_ref, wu_ref, wd_ref, o_ref, xb_ref, acc_ref):
    i = pl.program_id(0)
    j = pl.program_id(1)
    last = pl.num_programs(1) - 1
    active = i < nu_ref[0]

    @pl.when(active & (j == 0))
    def _():
        xb_ref[...] = xs_ref[...].astype(BF16)
        acc_ref[...] = jnp.zeros_like(acc_ref)

    @pl.when(active)
    def _():
        xb = xb_ref[...]
        g = jnp.dot(xb, wg_ref[0], preferred_element_type=F32)
        u = jnp.dot(xb, wu_ref[0], preferred_element_type=F32)
        a = (g * _sigmoid(g) * u).astype(BF16)
        acc_ref[...] += jnp.dot(a, wd_ref[0], preferred_element_type=F32)

    @pl.when(active & (j == last))
    def _():
        o_ref[...] = acc_ref[...]

    @pl.when(jnp.logical_not(active) & (j == last))
    def _():
        o_ref[...] = jnp.zeros_like(o_ref)


def _experts(xs, block_e, n_used, wg, wu, wd):
    n_slots, D = xs.shape
    E, _, F = wg.shape
    tm = TILES["exp_tm"]
    tf = min(TILES["exp_tf"], F)
    nb = n_slots // tm
    nj = F // tf

    def clamp(i, nu):
        return jnp.minimum(i, nu[0] - 1)

    def x_map(i, j, be, nu):
        return (clamp(i, nu), 0)

    def w_col_map(i, j, be, nu):
        return (be[clamp(i, nu)], 0, jnp.where(i < nu[0], j, nj - 1))

    def w_row_map(i, j, be, nu):
        return (be[clamp(i, nu)], jnp.where(i < nu[0], j, nj - 1), 0)

    return pl.pallas_call(
        _expert_kernel,
        out_shape=jax.ShapeDtypeStruct((n_slots, D), F32),
        grid_spec=pltpu.PrefetchScalarGridSpec(
            num_scalar_prefetch=2,
            grid=(nb, nj),
            in_specs=[pl.BlockSpec((tm, D), x_map),
                      pl.BlockSpec((1, D, tf), w_col_map),
                      pl.BlockSpec((1, D, tf), w_col_map),
                      pl.BlockSpec((1, tf, D), w_row_map)],
            out_specs=pl.BlockSpec((tm, D), lambda i, j, be, nu: (i, 0)),
            scratch_shapes=[pltpu.VMEM((tm, D), BF16), pltpu.VMEM((tm, D), F32)]),
        compiler_params=_cparams(("arbitrary", "arbitrary")),
        name="moe_experts",
    )(block_e, n_used, xs, wg, wu, wd)


def _combine_kernel(dest_ref, x_ref, wts_ref, gain_ref, gate_ref, ys_ref, o_ref, buf_ref, sem):
    tm = x_ref.shape[0]

    def issue(r, _):
        for kk in range(TOP_K):
            _row_copy(ys_ref, dest_ref[0, 0, TOP_K * r + kk], buf_ref.at[kk], r, sem).start()
        return 0

    lax.fori_loop(0, tm, issue, 0, unroll=ROW_DMA_UNROLL)
    for kk in range(TOP_K):
        pltpu.make_async_copy(ys_ref.at[pl.ds(0, tm)], buf_ref.at[kk], sem).wait()
    w = wts_ref[...]
    y = w[:, 0:1] * buf_ref[0] + w[:, 1:2] * buf_ref[1]
    o_ref[...] = x_ref[...] + gate_ref[0] * _rms(y, gain_ref[...])


def _combine(x2, ys, dest, wts, gain, gate, S):
    T, D = x2.shape
    tm = min(TILES["disp_tm"], S)
    spb = S // tm
    nt = T // tm
    return pl.pallas_call(
        _combine_kernel,
        out_shape=jax.ShapeDtypeStruct((T, D), F32),
        grid=(nt,),
        in_specs=[pl.BlockSpec((1, 1, TOP_K * tm), lambda i: (i, 0, 0), memory_space=pltpu.SMEM),
                  pl.BlockSpec((tm, D), lambda i: (i, 0)),
                  pl.BlockSpec((tm, LANES), lambda i: (i, 0)),
                  pl.BlockSpec((1, D), lambda i: (0, 0)),
                  pl.BlockSpec((1, 1, D), lambda i: (i // spb, 0, 0)),
                  pl.BlockSpec(memory_space=pl.ANY)],
        out_specs=pl.BlockSpec((tm, D), lambda i: (i, 0)),
        scratch_shapes=[pltpu.VMEM((TOP_K, tm, D), F32), pltpu.SemaphoreType.DMA],
        compiler_params=_cparams(("arbitrary",)),
        name="moe_combine",
    )(dest.reshape(nt, 1, TOP_K * tm), x2, wts, gain.reshape(1, D), gate, ys)


def _moe(x2, gain_pre, sc, sh, router_w, router_b, wg, wu, wd, gain_post, gate, S):
    T, D = x2.shape
    E = router_w.shape[1]
    tm_e = TILES["exp_tm"]
    rw = jnp.zeros((D, LANES), F32).at[:, :E].set(router_w)
    rb = jnp.full((1, LANES), NEG_BIG, F32).at[0, :E].set(router_b)
    h, meta, wts, cnt = _router(x2, gain_pre, sc, sh, rw, rb, S)
    counts = cnt[0, :E].astype(jnp.int32)
    padded = ((counts + tm_e - 1) // tm_e) * tm_e
    pend = jnp.cumsum(padded)
    pstart = pend - padded
    dest = pstart[meta[:, 0:TOP_K]] + meta[:, TOP_K:2 * TOP_K]
    n_slots = T * TOP_K + E * tm_e
    nb = n_slots // tm_e
    starts = jnp.arange(nb, dtype=jnp.int32) * tm_e
    block_e = jnp.minimum(jnp.sum((starts[:, None] >= pend[None, :]).astype(jnp.int32), axis=1), E - 1)
    n_used = (pend[-1:] // tm_e).astype(jnp.int32)
    xs = _dispatch(h, dest.reshape(-1), n_slots)
    ys = _experts(xs, block_e, n_used, wg, wu, wd)
    return _combine(x2, ys, dest.reshape(-1), wts, gain_post, gate, S)


def _prep_in_proj(w_in, b_in):
    q_scale = np.float32((MLA_NOPE + MLA_ROPE) ** -0.5 * np.log2(np.e))
    k_scale = np.float32(ML_QK ** -0.5)

    def build(a):
        lead = a.shape[:-1]

        def seg(name):
            s = IN_START[name]
            return a[..., s:s + IN_SIZE[name]]

        mq = seg("mq").reshape(lead + (MLA_HEADS, MLA_NOPE + MLA_ROPE)) * q_scale
        parts = {
            "qn": mq[..., :MLA_NOPE].reshape(lead + (MLA_HEADS * MLA_NOPE,)),
            "qr": mq[..., MLA_NOPE:].reshape(lead + (MLA_HEADS * MLA_ROPE,)),
            "kr": jnp.concatenate([seg("mkr"), jnp.zeros(lead + (LANES - MLA_ROPE,), a.dtype)], axis=-1),
            "gates": jnp.concatenate([seg("li"), seg("lf"),
                                      jnp.zeros(lead + (LANES - 2 * ML_HEADS,), a.dtype)], axis=-1),
            "lk": seg("lk") * k_scale,
            "ckv": seg("mckv"),
            "pad": jnp.zeros(lead + (Z_W["pad"],), a.dtype),
        }
        cols = [parts[n] if n in parts else seg(n) for n, _ in Z_ORDER]
        return jnp.concatenate(cols, axis=-1)

    return build(w_in).astype(BF16), build(b_in)


def kernel(x, c, positions, ada_w, ada_b, norm_pre_mix, norm_post_mix, norm_pre_ffn, norm_post_ffn, w_in, b_in,
           conv_w, sg_norm, sg_w, sg_b, mla_kv_norm, mla_w_uk, mla_w_uv, ml_norm, w_conv_out, w_sg_out,
           w_mla_out, w_ml_out, w_mix_out, ffn_w_gate, ffn_w_up, ffn_w_down, router_w, router_b,
           exp_w_gate, exp_w_up, exp_w_down):
    B, S, D = x.shape
    T = B * S
    x2 = x.reshape(T, D)
    pos = positions.reshape(T, 1).astype(jnp.int32)
    half = MLA_ROPE // 2
    freq32 = ROPE_THETA ** (-jnp.arange(half, dtype=F32) / half)
    freq = jnp.tile(freq32, LANES // half).reshape(1, LANES)
    mod = _ada_mod(c, ada_w, ada_b)
    for l in range(DEPTH):
        sh1, sc1, g1, sh2, sc2, g2 = [m.reshape(B, 1, D) for m in jnp.split(mod[l], 6, axis=-1)]
        w_p, b_p = _prep_in_proj(w_in[l], b_in[l])
        z, zg = _in_proj(x2, norm_pre_mix[l], sc1, sh1, w_p, b_p, S)
        q, k, v = _mla_prep(z, pos, freq, mla_kv_norm[l],
                            mla_w_uk[l].reshape(MLA_KV_RANK, -1).T.astype(BF16),
                            mla_w_uv[l].reshape(MLA_KV_RANK, -1).astype(BF16), B, S)
        y_mla = _attention(q, k, v).reshape(T, MLA_HEADS * MLA_V)
        y_ml = _mlstm(z, zg, ml_norm[l], B, S)
        merged = _merge(z, y_mla, y_ml, conv_w[l], sg_norm[l], sg_w[l], sg_b[l].T,
                        w_conv_out[l].astype(BF16), w_sg_out[l].astype(BF16),
                        w_mla_out[l].astype(BF16), w_ml_out[l].astype(BF16), S)
        x2 = _mix_out(x2, merged, w_mix_out[l].astype(BF16), norm_post_mix[l], g1, S)
        if l % 2 == 0:
            x2 = _ffn_dense(x2, norm_pre_ffn[l], sc2, sh2, ffn_w_gate[l // 2].astype(BF16),
                            ffn_w_up[l // 2].astype(BF16), ffn_w_down[l // 2].astype(BF16),
                            norm_post_ffn[l], g2, S)
        else:
            x2 = _moe(x2, norm_pre_ffn[l], sc2, sh2, router_w[l // 2], router_b[l // 2],
                      exp_w_gate[l // 2].astype(BF16), exp_w_up[l // 2].astype(BF16),
                      exp_w_down[l // 2].astype(BF16), norm_post_ffn[l], g2, S)
    return x2.reshape(B, S, D)
```

```python
import functools

import jax
import jax.numpy as jnp
import numpy as np
from jax import lax
from jax.experimental import pallas as pl
from jax.experimental.pallas import tpu as pltpu

F32 = jnp.float32
BF16 = jnp.bfloat16

D_MODEL = 2048
DEPTH = 2
CHUNK = 64
EPS = 1e-6
CONV_W = 1024
CONV_K = 3
SG_W = 1024
SG_BLOCK = 128
SG_GROUPS = 8
SG_GD = SG_W // SG_GROUPS
MLA_HEADS = 16
MLA_NOPE = 128
MLA_ROPE = 64
MLA_V = 128
MLA_KV_RANK = 512
ROPE_THETA = 10000.0
ML_HEADS = 4
ML_QK = 128
ML_V = 256
D_FF = 5632
N_EXPERTS = 8
TOP_K = 2
D_FF_EXPERT = 7168
IN_SIZES = (
    CONV_W, CONV_W, CONV_W, SG_W, SG_W,
    MLA_HEADS * (MLA_NOPE + MLA_ROPE), MLA_KV_RANK, MLA_ROPE,
    ML_HEADS * ML_QK, ML_HEADS * ML_QK, ML_HEADS * ML_V, ML_HEADS * ML_V, ML_HEADS, ML_HEADS,
    D_MODEL, D_MODEL, D_MODEL, D_MODEL,
)
IN_NAMES = ("cb", "cc", "ch", "su", "sv", "mq", "mckv", "mkr", "lq", "lk", "lv", "lo", "li", "lf",
            "g_conv", "g_sg", "g_mla", "g_ml")
IN_START = {n: sum(IN_SIZES[:j]) for j, n in enumerate(IN_NAMES)}
IN_SIZE = dict(zip(IN_NAMES, IN_SIZES))

LANES = 128
SUBLANES = 8
VMEM_LIMIT = 56 * 1024 * 1024
NEG_BIG = -1e30

Z_ORDER = (("qn", 2048), ("g_conv", 2048), ("g_sg", 2048), ("g_mla", 2048), ("g_ml", 2048),
           ("cb", 1024), ("cc", 1024), ("ch", 1024), ("su", 1024), ("sv", 1024), ("qr", 1024),
           ("lv", 1024), ("lo", 1024), ("ckv", 512), ("lq", 512), ("lk", 512),
           ("kr", 128), ("gates", 128), ("pad", 256))
Z_OFF = {}
_o = 0
for _n, _w in Z_ORDER:
    Z_OFF[_n] = _o
    _o += _w
NZ = _o
Z_W = dict(Z_ORDER)

TILES = dict(
    in_tm=1024, in_tn=2048,
    prep_tm=512,
    attn_tq=1024,
    ml_tm=512, ml_chunk=128,
    m1_tm=256,
    m2_tm=512,
    ffn_tm=512, ffn_tf=512,
    rt_tm=512,
    disp_tm=256,
    exp_tm=512, exp_tf=1024,
    ada_tn=1024,
)


def _cparams(sem, vmem=VMEM_LIMIT):
    return pltpu.CompilerParams(dimension_semantics=sem, vmem_limit_bytes=vmem)


def _sigmoid(x):
    return 1.0 / (1.0 + jnp.exp(-x))


def _gelu_tanh(x):
    return 0.5 * x * (1.0 + jnp.tanh(np.float32(np.sqrt(2.0 / np.pi)) * (x + 0.044715 * (x * x * x))))


def _rms(x, gain):
    return x * lax.rsqrt(jnp.mean(x * x, axis=-1, keepdims=True) + EPS) * gain


def _ada_kernel(c_ref, w_ref, b_ref, o_ref):
    c = c_ref[...]
    cs = (c * _sigmoid(c)).astype(BF16)
    o_ref[0] = jnp.dot(cs, w_ref[0].astype(BF16), preferred_element_type=F32) + b_ref[0]


def _ada_mod(c, ada_w, ada_b):
    L, D, N = ada_w.shape
    B = c.shape[0]
    tn = min(TILES["ada_tn"], N)
    c_pad = jnp.zeros((SUBLANES, D), F32).at[:B].set(c)
    out = pl.pallas_call(
        _ada_kernel,
        out_shape=jax.ShapeDtypeStruct((L, SUBLANES, N), F32),
        grid=(L, N // tn),
        in_specs=[pl.BlockSpec((SUBLANES, D), lambda l, j: (0, 0)),
                  pl.BlockSpec((1, D, tn), lambda l, j: (l, 0, j)),
                  pl.BlockSpec((1, 1, tn), lambda l, j: (l, 0, j))],
        out_specs=pl.BlockSpec((1, SUBLANES, tn), lambda l, j: (l, 0, j)),
        compiler_params=_cparams(("arbitrary", "arbitrary")),
        name="ada_mod",
    )(c_pad, ada_w, ada_b.reshape(L, 1, N))
    return out[:, :B]


def _in_proj_kernel(x_ref, g_ref, sc_ref, sh_ref, w_ref, b_ref, o_ref, og_ref, h_ref, *, gates_tile, gates_off):
    tm, tn = o_ref.shape
    slab = min(ROW_SLAB, tm)

    @pl.when(pl.program_id(1) == 0)
    def _():
        for r in range(tm // slab):
            rows = slice(r * slab, (r + 1) * slab)
            y = _rms(x_ref[rows, :], g_ref[...])
            h_ref[rows, :] = (y * (1.0 + sc_ref[0]) + sh_ref[0]).astype(BF16)

    h = h_ref[...]
    step = min(DOWN_PROJ_COLS, tn)
    for c in range(tn // step):
        cols = slice(c * step, (c + 1) * step)
        acc = jnp.dot(h, w_ref[:, cols], preferred_element_type=F32) + b_ref[:, cols]
        o_ref[:, cols] = acc.astype(o_ref.dtype)
        if c == gates_off // step:
            @pl.when(pl.program_id(1) == gates_tile)
            def _(acc=acc):
                og_ref[...] = acc[:, gates_off % step:gates_off % step + LANES]


def _in_proj(x2, gain, sc, sh, w, b, S):
    T, D = x2.shape
    N = w.shape[1]
    tm = min(TILES["in_tm"], S)
    tn = min(TILES["in_tn"], N)
    spb = S // tm
    kern = functools.partial(_in_proj_kernel, gates_tile=Z_OFF["gates"] // tn, gates_off=Z_OFF["gates"] % tn)
    return pl.pallas_call(
        kern,
        out_shape=(jax.ShapeDtypeStruct((T, N), BF16), jax.ShapeDtypeStruct((T, LANES), F32)),
        grid=(T // tm, N // tn),
        in_specs=[pl.BlockSpec((tm, D), lambda i, j: (i, 0)),
                  pl.BlockSpec((1, D), lambda i, j: (0, 0)),
                  pl.BlockSpec((1, 1, D), lambda i, j: (i // spb, 0, 0)),
                  pl.BlockSpec((1, 1, D), lambda i, j: (i // spb, 0, 0)),
                  pl.BlockSpec((D, tn), lambda i, j: (0, j)),
                  pl.BlockSpec((1, tn), lambda i, j: (0, j))],
        out_specs=(pl.BlockSpec((tm, tn), lambda i, j: (i, j)),
                   pl.BlockSpec((tm, LANES), lambda i, j: (i, 0))),
        scratch_shapes=[pltpu.VMEM((tm, D), BF16)],
        compiler_params=_cparams(("arbitrary", "arbitrary")),
        name="in_proj",
    )(x2, gain.reshape(1, D), sc, sh, w, b.reshape(1, N))


def _swap_halves(v, first_half):
    return jnp.where(first_half, pltpu.roll(v, LANES - MLA_ROPE // 2, 1), pltpu.roll(v, MLA_ROPE // 2, 1))


def _mla_prep_kernel(qn_ref, qr_ref, ckv_ref, kr_ref, pos_ref, freq_ref, kvn_ref, wuk_ref, wuv_ref,
                     q_ref, k_ref, v_ref):
    tm = qn_ref.shape[0]
    ang = pos_ref[...].astype(F32) * freq_ref[...]
    cos = jnp.cos(ang)
    sin = jnp.sin(ang)
    lane = lax.broadcasted_iota(jnp.int32, (1, LANES), 1)
    first_half = (lane % MLA_ROPE) < (MLA_ROPE // 2)
    sgn_sin = jnp.where(first_half, -sin, sin)

    for c in range(MLA_HEADS // 2):
        v = qr_ref[:, c * LANES:(c + 1) * LANES].astype(F32)
        r = (v * cos + _swap_halves(v, first_half) * sgn_sin).astype(BF16)
        for h in (2 * c, 2 * c + 1):
            q_ref[0, h, :, 0:MLA_NOPE] = qn_ref[:, h * MLA_NOPE:(h + 1) * MLA_NOPE]
            q_ref[0, h, :, MLA_NOPE:2 * MLA_NOPE] = r

    cn = _rms(ckv_ref[...].astype(F32), kvn_ref[...]).astype(BF16)
    knt = lax.dot_general(wuk_ref[...], cn, (((1,), (1,)), ((), ())), preferred_element_type=F32).astype(BF16)
    vv = jnp.dot(cn, wuv_ref[...], preferred_element_type=F32).astype(BF16)
    ka = kr_ref[...].astype(F32)
    kr_even = ka * cos + _swap_halves(ka, first_half) * sgn_sin
    kr_odd = pltpu.roll(kr_even, MLA_ROPE, 1)
    krt_even = kr_even.T.astype(BF16)
    krt_odd = kr_odd.T.astype(BF16)
    ones_col = jnp.where(lax.broadcasted_iota(jnp.int32, (tm, LANES), 1) == 0, 1.0, 0.0).astype(BF16)
    for h in range(MLA_HEADS):
        k_ref[0, h, 0:MLA_NOPE, :] = knt[h * MLA_NOPE:(h + 1) * MLA_NOPE, :]
        k_ref[0, h, MLA_NOPE:2 * MLA_NOPE, :] = krt_even if h % 2 == 0 else krt_odd
        v_ref[0, h, :, 0:MLA_V] = vv[:, h * MLA_V:(h + 1) * MLA_V]
        v_ref[0, h, :, MLA_V:MLA_V + LANES] = ones_col


def _mla_prep(z, pos, freq, kv_norm, w_uk_t, w_uv, B, S):
    tm = min(TILES["prep_tm"], S)
    spb = S // tm
    H = MLA_HEADS
    KD = 2 * MLA_NOPE

    def zspec(name):
        w = Z_W[name]
        cb = Z_OFF[name] // w
        return pl.BlockSpec((tm, w), lambda b, i: (b * spb + i, cb))

    return pl.pallas_call(
        _mla_prep_kernel,
        out_shape=(jax.ShapeDtypeStruct((B, H, S, KD), BF16),
                   jax.ShapeDtypeStruct((B, H, KD, S), BF16),
                   jax.ShapeDtypeStruct((B, H, S, MLA_V + LANES), BF16)),
        grid=(B, spb),
        in_specs=[zspec("qn"), zspec("qr"), zspec("ckv"), zspec("kr"),
                  pl.BlockSpec((tm, 1), lambda b, i: (b * spb + i, 0)),
                  pl.BlockSpec((1, LANES), lambda b, i: (0, 0)),
                  pl.BlockSpec((1, MLA_KV_RANK), lambda b, i: (0, 0)),
                  pl.BlockSpec((H * MLA_NOPE, MLA_KV_RANK), lambda b, i: (0, 0)),
                  pl.BlockSpec((MLA_KV_RANK, H * MLA_V), lambda b, i: (0, 0))],
        out_specs=(pl.BlockSpec((1, H, tm, KD), lambda b, i: (b, 0, i, 0)),
                   pl.BlockSpec((1, H, KD, tm), lambda b, i: (b, 0, 0, i)),
                   pl.BlockSpec((1, H, tm, MLA_V + LANES), lambda b, i: (b, 0, i, 0))),
        compiler_params=_cparams(("arbitrary", "arbitrary")),
        name="mla_prep",
    )(z, z, z, z, pos, freq, kv_norm.reshape(1, -1), w_uk_t, w_uv)


ATTN_GROUPS = 2
ATTN_STRIP = 32
ATTN_KTILE = 256
ATTN_PAIRS_PER_TRIP = 2


def _attn_kernel(q_ref, kt_ref, v_ref, o_ref, s0_ref, s1_ref, *scratch):
    G = ATTN_GROUPS
    p_refs, acc_refs, m_refs, bm_refs, al_refs = (scratch[n * G:(n + 1) * G] for n in range(5))
    tq = q_ref.shape[2]
    tk = tq // G
    va = v_ref.shape[3]
    i = pl.program_id(2)
    for g in range(G):
        m_refs[g][...] = jnp.full(m_refs[g].shape, NEG_BIG, F32)
        acc_refs[g][...] = jnp.zeros(acc_refs[g].shape, F32)

    def scores(blk, s_ref, first_row=0):
        start = pl.multiple_of(blk * tk, tk)
        s_ref[first_row:, :] = jnp.dot(q_ref[0, 0, first_row:, :], kt_ref[0, 0, :, pl.ds(start, tk)],
                                       preferred_element_type=F32)

    def softmax_pv(blk, s_ref, modes):
        v = v_ref[0, 0, pl.ds(pl.multiple_of(blk * tk, tk), tk), :]
        for g in range(G):
            if modes[g] == "skip":
                continue
            p_ref, acc_ref, m_ref, bm_ref, al_ref = p_refs[g], acc_refs[g], m_refs[g], bm_refs[g], al_refs[g]

            def load_strip(r, cols=slice(0, tk)):
                x = s_ref[g * tk + r * ATTN_STRIP:g * tk + (r + 1) * ATTN_STRIP, cols]
                if modes[g] == "diag":
                    row = (lax.broadcasted_iota(jnp.int32, x.shape, 0) + r * ATTN_STRIP) // CHUNK
                    col = (lax.broadcasted_iota(jnp.int32, x.shape, 1) + cols.start) // CHUNK
                    x = jnp.where(col <= row, x, NEG_BIG)
                return x

            for r in range(tk // ATTN_STRIP):
                rows = slice(r * ATTN_STRIP, (r + 1) * ATTN_STRIP)
                bm_ref[rows, :] = jnp.broadcast_to(jnp.max(load_strip(r), axis=-1, keepdims=True),
                                                   (ATTN_STRIP, LANES))
            m_old = m_ref[...]
            m_new = jnp.maximum(m_old, bm_ref[...])
            m_ref[...] = m_new
            al_ref[...] = jnp.exp2(m_old - m_new)
            pv = acc_ref[...] * jnp.concatenate([al_ref[...]] * (va // LANES), axis=1)
            kt = min(ATTN_KTILE, tk)
            for c in range(tk // kt):
                cols = slice(c * kt, (c + 1) * kt)
                for r in range(tk // ATTN_STRIP):
                    rows = slice(r * ATTN_STRIP, (r + 1) * ATTN_STRIP)
                    p = jnp.exp2(load_strip(r, cols) - jnp.concatenate([m_ref[rows, :]] * (kt // LANES), axis=1))
                    p_ref[rows, cols] = p.astype(BF16)
                pv = pv + jnp.dot(p_ref[:, cols], v[cols, :], preferred_element_type=F32)
            acc_ref[...] = pv

    scores(0, s0_ref)

    def visible_pair(t):
        scores(2 * t + 1, s1_ref)
        softmax_pv(2 * t, s0_ref, ("full", "full"))
        scores(2 * t + 2, s0_ref)
        softmax_pv(2 * t + 1, s1_ref, ("full", "full"))

    def body(u, carry):
        for n in range(ATTN_PAIRS_PER_TRIP):
            visible_pair(ATTN_PAIRS_PER_TRIP * u + n)
        return carry

    lax.fori_loop(0, i // ATTN_PAIRS_PER_TRIP, body, 0)
    for n in range(1, ATTN_PAIRS_PER_TRIP):
        @pl.when(i % ATTN_PAIRS_PER_TRIP >= n)
        def _(n=n):
            visible_pair(i - i % ATTN_PAIRS_PER_TRIP + n - 1)

    scores(2 * i + 1, s1_ref, first_row=tk)
    softmax_pv(2 * i, s0_ref, ("diag", "full"))
    softmax_pv(2 * i + 1, s1_ref, ("skip", "diag"))

    for g in range(G):
        acc = acc_refs[g][...]
        o_ref[0, g * tk:(g + 1) * tk, :] = (acc[:, :MLA_V] / acc[:, MLA_V:MLA_V + 1]).astype(o_ref.dtype)


def _attention(q, kt, v):
    B, H, S, KD = q.shape
    VA = v.shape[3]
    tq = min(TILES["attn_tq"], S)
    hq = tq // ATTN_GROUPS
    assert hq % CHUNK == 0 and hq % ATTN_STRIP == 0 and S % tq == 0
    return pl.pallas_call(
        _attn_kernel,
        out_shape=jax.ShapeDtypeStruct((B, S, H * MLA_V), BF16),
        grid=(B, H, S // tq),
        in_specs=[pl.BlockSpec((1, 1, tq, KD), lambda b, h, i: (b, h, i, 0)),
                  pl.BlockSpec((1, 1, KD, S), lambda b, h, i: (b, h, 0, 0)),
                  pl.BlockSpec((1, 1, S, VA), lambda b, h, i: (b, h, 0, 0))],
        out_specs=pl.BlockSpec((1, tq, MLA_V), lambda b, h, i: (b, i, h)),
        scratch_shapes=([pltpu.VMEM((tq, hq), F32)] * 2 + [pltpu.VMEM((hq, hq), BF16)] * ATTN_GROUPS
                        + [pltpu.VMEM((hq, VA), F32)] * ATTN_GROUPS
                        + [pltpu.VMEM((hq, LANES), F32)] * (3 * ATTN_GROUPS)),
        compiler_params=_cparams(("arbitrary", "arbitrary", "arbitrary")),
        name="mla_attention",
    )(q, kt, v)


ML_VA = ML_V + LANES


def _log_sigmoid(x):
    return -(jnp.maximum(-x, 0.0) + jnp.log1p(jnp.exp(-jnp.abs(x))))


def _mlstm_kernel(q_ref, k_ref, v_ref, o_ref, g_ref, gain_ref, y_ref, *state):
    L = TILES["ml_chunk"]
    nb, tm = q_ref.shape[0], q_ref.shape[1]
    assert L == LANES and tm % L == 0
    st_refs, m_refs = state[:nb * ML_HEADS], state[nb * ML_HEADS:]

    @pl.when(pl.program_id(0) == 0)
    def _():
        for r in st_refs + m_refs:
            r[...] = jnp.zeros_like(r)

    row = lax.broadcasted_iota(jnp.int32, (L, L), 0)
    col = lax.broadcasted_iota(jnp.int32, (L, L), 1)
    causal = col <= row
    tril = jnp.where(causal, 1.0, 0.0).astype(F32)
    triu = jnp.where(row <= col, 1.0, 0.0).astype(F32)
    lane_a = lax.broadcasted_iota(jnp.int32, (L, LANES), 1)
    ones_col = jnp.where(lane_a == 0, 1.0, 0.0).astype(BF16)

    def chunk(c, _):
        r0 = pl.multiple_of(c * L, L)
        for b in range(nb):
            chunk_one_batch(b, r0)
        return 0

    def chunk_one_batch(b, r0):
        gf = g_ref[b, pl.ds(r0, L), :]
        gt = gf.T
        lf_c = _log_sigmoid(gf)
        b_c = jnp.dot(tril, lf_c, precision=lax.Precision.HIGHEST, preferred_element_type=F32)
        lf_r = _log_sigmoid(gt)
        b_r = jnp.dot(lf_r, triu, precision=lax.Precision.HIGHEST, preferred_element_type=F32)
        for h in range(ML_HEADS):
            sh = b * ML_HEADS + h
            q = q_ref[b, pl.ds(r0, L), h * ML_QK:(h + 1) * ML_QK]
            k = k_ref[b, pl.ds(r0, L), h * ML_QK:(h + 1) * ML_QK]
            v = v_ref[b, pl.ds(r0, L), h * ML_V:(h + 1) * ML_V]
            vaug = jnp.concatenate([v, ones_col], axis=1)
            bcol = b_c[:, ML_HEADS + h:ML_HEADS + h + 1]
            icol = gf[:, h:h + 1]
            brow = b_r[ML_HEADS + h:ML_HEADS + h + 1, :]
            irow = gt[h:h + 1, :]
            m0 = m_refs[sh][0:1, 0:1]
            dl = jnp.where(causal, bcol - brow + irow, NEG_BIG)
            inter_log = bcol + m0
            m_t = jnp.maximum(inter_log, jnp.max(dl, axis=-1, keepdims=True))
            dm = jnp.exp(dl - m_t)
            s = lax.dot_general(q, k, (((1,), (1,)), ((), ())), preferred_element_type=F32)
            p = (s * dm).astype(BF16)
            st = st_refs[sh][...]
            intra = jnp.dot(p, vaug, preferred_element_type=F32)
            inter = jnp.dot(q, st.astype(BF16), preferred_element_type=F32)
            tot = intra + jnp.exp(inter_log - m_t) * inter
            num = tot[:, :ML_V]
            den = tot[:, ML_V:ML_V + 1]
            hv = num / jnp.maximum(jnp.abs(den), jnp.exp(-m_t))
            hn = _rms(hv, gain_ref[:, h * ML_V:(h + 1) * ML_V])
            og = o_ref[b, pl.ds(r0, L), h * ML_V:(h + 1) * ML_V].astype(F32)
            y_ref[b, pl.ds(r0, L), h * ML_V:(h + 1) * ML_V] = (_sigmoid(og) * hn).astype(y_ref.dtype)
            b_end = bcol[L - 1:L, :]
            a = b_end - bcol + icol
            g = jnp.max(a, axis=0, keepdims=True)
            wa = jnp.exp(a - g)
            kt = k.astype(F32).T.astype(BF16)
            upd = jnp.dot(kt, (vaug.astype(F32) * wa).astype(BF16), preferred_element_type=F32)
            m_new = jnp.maximum(b_end + m0, g)
            decay = jnp.exp(b_end + m0 - m_new)
            inject = jnp.exp(g - m_new)
            st_refs[sh][...] = decay * st + inject * upd
            m_refs[sh][...] = jnp.broadcast_to(m_new, (SUBLANES, LANES))

    lax.fori_loop(0, tm // L, chunk, 0)


def _mlstm(z, zg, ml_norm, B, S):
    tm = min(TILES["ml_tm"], S)
    T = B * S
    z3 = z.reshape(B, S, z.shape[1])

    def zspec(name):
        w = Z_W[name]
        cb = Z_OFF[name] // w
        return pl.BlockSpec((B, tm, w), lambda i: (0, i, cb))

    y = pl.pallas_call(
        _mlstm_kernel,
        out_shape=jax.ShapeDtypeStruct((B, S, ML_HEADS * ML_V), BF16),
        grid=(S // tm,),
        in_specs=[zspec("lq"), zspec("lk"), zspec("lv"), zspec("lo"),
                  pl.BlockSpec((B, tm, LANES), lambda i: (0, i, 0)),
                  pl.BlockSpec((1, ML_HEADS * ML_V), lambda i: (0, 0))],
        out_specs=pl.BlockSpec((B, tm, ML_HEADS * ML_V), lambda i: (0, i, 0)),
        scratch_shapes=([pltpu.VMEM((ML_QK, ML_VA), F32)] * (B * ML_HEADS)
                        + [pltpu.VMEM((SUBLANES, LANES), F32)] * (B * ML_HEADS)),
        compiler_params=_cparams(("arbitrary",)),
        name="mlstm",
    )(z3, z3, z3, z3, zg.reshape(B, S, LANES), ml_norm.reshape(1, -1))
    return y.reshape(T, ML_HEADS * ML_V)


def _merge_kernel(cb_ref, cc_ref, ch_ref, ccp_ref, chp_ref, su_ref, sv_ref,
                  gc_ref, gs_ref, gm_ref, gl_ref, ymla_ref, yml_ref,
                  convw_ref, sgn_ref, sgw_ref, sgb_ref, wc_ref, ws_ref, wm_ref, wl_ref,
                  o_ref, *, steps_per_batch):
    tm = cb_ref.shape[0]
    i = pl.program_id(0)
    z = cc_ref[...].astype(F32) * ch_ref[...].astype(F32)
    zp = ccp_ref[...].astype(F32) * chp_ref[...].astype(F32)
    zp = jnp.where(i % steps_per_batch == 0, 0.0, zp)
    rid = lax.broadcasted_iota(jnp.int32, (tm, 1), 0)
    z1 = jnp.where(rid == 0, zp[SUBLANES - 1:SUBLANES, :], pltpu.roll(z, 1, 0))
    z2 = jnp.where(rid == 0, zp[SUBLANES - 2:SUBLANES - 1, :],
                   jnp.where(rid == 1, zp[SUBLANES - 1:SUBLANES, :], pltpu.roll(z, 2, 0)))
    cw = convw_ref[...]
    y_conv = cb_ref[...].astype(F32) * (cw[0:1, :] * z2 + cw[1:2, :] * z1 + cw[2:3, :] * z)
    acc = _sigmoid(gc_ref[...].astype(F32)) * jnp.dot(y_conv.astype(BF16), wc_ref[...],
                                                      preferred_element_type=F32)
    u = _gelu_tanh(su_ref[...].astype(F32))
    vn = _rms(_gelu_tanh(sv_ref[...].astype(F32)), sgn_ref[...]).astype(BF16)
    r = lax.broadcasted_iota(jnp.int32, (SG_BLOCK, SG_BLOCK), 0)
    c = lax.broadcasted_iota(jnp.int32, (SG_BLOCK, SG_BLOCK), 1)
    blocks = []
    for n in range(tm // SG_BLOCK):
        cols = []
        for g in range(SG_GROUPS):
            w = jnp.where(c <= r, sgw_ref[g], 0.0).astype(BF16)
            vb = vn[n * SG_BLOCK:(n + 1) * SG_BLOCK, g * SG_GD:(g + 1) * SG_GD]
            cols.append(jnp.dot(w, vb, preferred_element_type=F32) + sgb_ref[:, g:g + 1])
        blocks.append(jnp.concatenate(cols, axis=1))
    mixed = blocks[0] if len(blocks) == 1 else jnp.concatenate(blocks, axis=0)
    y_sg = (u * mixed).astype(BF16)
    acc += _sigmoid(gs_ref[...].astype(F32)) * jnp.dot(y_sg, ws_ref[...], preferred_element_type=F32)
    acc += _sigmoid(gm_ref[...].astype(F32)) * jnp.dot(ymla_ref[...], wm_ref[...], preferred_element_type=F32)
    acc += _sigmoid(gl_ref[...].astype(F32)) * jnp.dot(yml_ref[...], wl_ref[...], preferred_element_type=F32)
    o_ref[...] = acc.astype(o_ref.dtype)


def _merge(z, y_mla, y_ml, conv_w, sg_norm, sg_w, sg_bt, wc, ws, wm, wl, S):
    T = z.shape[0]
    D = wc.shape[1]
    tm = min(TILES["m1_tm"], S)
    spb = S // tm
    rows8 = tm // SUBLANES

    def zspec(name):
        w = Z_W[name]
        cb = Z_OFF[name] // w
        return pl.BlockSpec((tm, w), lambda i: (i, cb))

    def zprev(name):
        w = Z_W[name]
        cb = Z_OFF[name] // w
        return pl.BlockSpec((SUBLANES, w), lambda i: (jnp.maximum(i * rows8 - 1, 0), cb))

    def const(shape):
        nd = len(shape)
        return pl.BlockSpec(shape, lambda i: (0,) * nd, pipeline_mode=pl.Buffered(1))

    return pl.pallas_call(
        functools.partial(_merge_kernel, steps_per_batch=spb),
        out_shape=jax.ShapeDtypeStruct((T, D), BF16),
        grid=(T // tm,),
        in_specs=[zspec("cb"), zspec("cc"), zspec("ch"), zprev("cc"), zprev("ch"), zspec("su"), zspec("sv"),
                  zspec("g_conv"), zspec("g_sg"), zspec("g_mla"), zspec("g_ml"),
                  pl.BlockSpec((tm, y_mla.shape[1]), lambda i: (i, 0)),
                  pl.BlockSpec((tm, y_ml.shape[1]), lambda i: (i, 0)),
                  const(conv_w.shape), const((1, SG_W)), const(sg_w.shape), const(sg_bt.shape),
                  const(wc.shape), const(ws.shape), const(wm.shape), const(wl.shape)],
        out_specs=pl.BlockSpec((tm, D), lambda i: (i, 0)),
        compiler_params=_cparams(("arbitrary",)),
        name="merge",
    )(z, z, z, z, z, z, z, z, z, z, z, y_mla, y_ml,
      conv_w, sg_norm.reshape(1, -1), sg_w, sg_bt, wc, ws, wm, wl)


def _mix_out_kernel(x_ref, m_ref, w_ref, gain_ref, gate_ref, o_ref):
    y = jnp.dot(m_ref[...], w_ref[...], preferred_element_type=F32)
    o_ref[...] = x_ref[...] + gate_ref[0] * _rms(y, gain_ref[...])


def _mix_out(x2, merged, w, gain, gate, S):
    T, D = x2.shape
    tm = min(TILES["m2_tm"], S)
    spb = S // tm
    return pl.pallas_call(
        _mix_out_kernel,
        out_shape=jax.ShapeDtypeStruct((T, D), F32),
        grid=(T // tm,),
        in_specs=[pl.BlockSpec((tm, D), lambda i: (i, 0)),
                  pl.BlockSpec((tm, D), lambda i: (i, 0)),
                  pl.BlockSpec((D, D), lambda i: (0, 0), pipeline_mode=pl.Buffered(1)),
                  pl.BlockSpec((1, D), lambda i: (0, 0)),
                  pl.BlockSpec((1, 1, D), lambda i: (i // spb, 0, 0))],
        out_specs=pl.BlockSpec((tm, D), lambda i: (i, 0)),
        compiler_params=_cparams(("arbitrary",)),
        name="mix_out",
    )(x2, merged, w, gain.reshape(1, D), gate)


DOWN_PROJ_COLS = 512
ROW_SLAB = 256


def _accumulate_dot(o_ref, a, w_ref):
    n = o_ref.shape[1]
    step = min(DOWN_PROJ_COLS, n)
    for c in range(n // step):
        cols = slice(c * step, (c + 1) * step)
        w = w_ref[:, cols] if len(w_ref.shape) == 2 else w_ref[0, :, cols]
        o_ref[:, cols] += jnp.dot(a, w, preferred_element_type=F32)


def _ffn_kernel(x_ref, g_ref, sc_ref, sh_ref, wg_ref, wu_ref, wd_ref, gpost_ref, gate_ref, o_ref, h_ref):
    j = pl.program_id(1)

    tm = x_ref.shape[0]
    slab = min(ROW_SLAB, tm)

    @pl.when(j == 0)
    def _():
        for r in range(tm // slab):
            rows = slice(r * slab, (r + 1) * slab)
            y = _rms(x_ref[rows, :], g_ref[...])
            h_ref[rows, :] = (y * (1.0 + sc_ref[0]) + sh_ref[0]).astype(BF16)
        o_ref[...] = jnp.zeros_like(o_ref)

    h = h_ref[...]
    g = jnp.dot(h, wg_ref[...], preferred_element_type=F32)
    u = jnp.dot(h, wu_ref[...], preferred_element_type=F32)
    a = (g * _sigmoid(g) * u).astype(BF16)
    _accumulate_dot(o_ref, a, wd_ref)

    @pl.when(j == pl.num_programs(1) - 1)
    def _():
        for r in range(tm // slab):
            rows = slice(r * slab, (r + 1) * slab)
            o_ref[rows, :] = x_ref[rows, :] + gate_ref[0] * _rms(o_ref[rows, :], gpost_ref[...])


def _ffn_dense(x2, gain_pre, sc, sh, wg, wu, wd, gain_post, gate, S):
    T, D = x2.shape
    F = wg.shape[1]
    tm = min(TILES["ffn_tm"], S)
    tf = min(TILES["ffn_tf"], F)
    spb = S // tm
    return pl.pallas_call(
        _ffn_kernel,
        out_shape=jax.ShapeDtypeStruct((T, D), F32),
        grid=(T // tm, F // tf),
        in_specs=[pl.BlockSpec((tm, D), lambda i, j: (i, 0)),
                  pl.BlockSpec((1, D), lambda i, j: (0, 0)),
                  pl.BlockSpec((1, 1, D), lambda i, j: (i // spb, 0, 0)),
                  pl.BlockSpec((1, 1, D), lambda i, j: (i // spb, 0, 0)),
                  pl.BlockSpec((D, tf), lambda i, j: (0, j)),
                  pl.BlockSpec((D, tf), lambda i, j: (0, j)),
                  pl.BlockSpec((tf, D), lambda i, j: (j, 0)),
                  pl.BlockSpec((1, D), lambda i, j: (0, 0)),
                  pl.BlockSpec((1, 1, D), lambda i, j: (i // spb, 0, 0))],
        out_specs=pl.BlockSpec((tm, D), lambda i, j: (i, 0)),
        scratch_shapes=[pltpu.VMEM((tm, D), BF16)],
        compiler_params=_cparams(("arbitrary", "arbitrary")),
        name="ffn_dense",
    )(x2, gain_pre.reshape(1, D), sc, sh, wg, wu, wd, gain_post.reshape(1, D), gate)


def _router_kernel(x_ref, g_ref, sc_ref, sh_ref, rw_ref, rb_ref, h_ref, meta_ref, wts_ref, cnt_ref, carry_ref):
    tm = x_ref.shape[0]

    @pl.when(pl.program_id(0) == 0)
    def _():
        carry_ref[...] = jnp.zeros_like(carry_ref)

    h = _rms(x_ref[...], g_ref[...]) * (1.0 + sc_ref[0]) + sh_ref[0]
    h_ref[...] = h
    logits = jnp.dot(h, rw_ref[...], precision=lax.Precision.HIGHEST, preferred_element_type=F32) + rb_ref[...]
    lane = lax.broadcasted_iota(jnp.int32, (tm, LANES), 1)
    m1 = jnp.max(logits, axis=-1, keepdims=True)
    i1 = jnp.min(jnp.where(logits == m1, lane, LANES), axis=-1, keepdims=True)
    oh1 = lane == i1
    rest = jnp.where(oh1, 2.0 * NEG_BIG, logits)
    m2 = jnp.max(rest, axis=-1, keepdims=True)
    i2 = jnp.min(jnp.where(rest == m2, lane, LANES), axis=-1, keepdims=True)
    oh2 = lane == i2
    e2 = jnp.exp(m2 - m1)
    den = 1.0 + e2
    w1 = 1.0 / den
    w2 = e2 / den
    oh = jnp.where(oh1 | oh2, 1.0, 0.0)
    r = lax.broadcasted_iota(jnp.int32, (tm, tm), 0)
    c = lax.broadcasted_iota(jnp.int32, (tm, tm), 1)
    strict = jnp.where(c < r, 1.0, 0.0).astype(BF16)
    carry = carry_ref[0:1, :]
    cum = jnp.dot(strict, oh.astype(BF16), preferred_element_type=F32) + carry
    rank1 = jnp.sum(jnp.where(oh1, cum, 0.0), axis=-1, keepdims=True).astype(jnp.int32)
    rank2 = jnp.sum(jnp.where(oh2, cum, 0.0), axis=-1, keepdims=True).astype(jnp.int32)
    new_carry = carry + jnp.sum(oh, axis=0, keepdims=True)
    carry_ref[...] = jnp.broadcast_to(new_carry, carry_ref.shape)
    cnt_ref[...] = jnp.broadcast_to(new_carry, cnt_ref.shape)
    meta_ref[...] = jnp.where(lane == 0, i1, jnp.where(lane == 1, i2, jnp.where(lane == 2, rank1,
                              jnp.where(lane == 3, rank2, 0))))
    wts_ref[...] = jnp.where(lane == 0, w1, jnp.where(lane == 1, w2, 0.0))


def _router(x2, gain, sc, sh, rw, rb, S):
    T, D = x2.shape
    tm = min(TILES["rt_tm"], S)
    spb = S // tm
    return pl.pallas_call(
        _router_kernel,
        out_shape=(jax.ShapeDtypeStruct((T, D), F32),
                   jax.ShapeDtypeStruct((T, LANES), jnp.int32),
                   jax.ShapeDtypeStruct((T, LANES), F32),
                   jax.ShapeDtypeStruct((SUBLANES, LANES), F32)),
        grid=(T // tm,),
        in_specs=[pl.BlockSpec((tm, D), lambda i: (i, 0)),
                  pl.BlockSpec((1, D), lambda i: (0, 0)),
                  pl.BlockSpec((1, 1, D), lambda i: (i // spb, 0, 0)),
                  pl.BlockSpec((1, 1, D), lambda i: (i // spb, 0, 0)),
                  pl.BlockSpec((D, LANES), lambda i: (0, 0)),
                  pl.BlockSpec((1, LANES), lambda i: (0, 0))],
        out_specs=(pl.BlockSpec((tm, D), lambda i: (i, 0)),
                   pl.BlockSpec((tm, LANES), lambda i: (i, 0)),
                   pl.BlockSpec((tm, LANES), lambda i: (i, 0)),
                   pl.BlockSpec((SUBLANES, LANES), lambda i: (0, 0))),
        scratch_shapes=[pltpu.VMEM((SUBLANES, LANES), F32)],
        compiler_params=_cparams(("arbitrary",)),
        name="moe_router",
    )(x2, gain.reshape(1, D), sc, sh, rw, rb)


ROW_DMA_UNROLL = 8


def _row_copy(src_ref, src_row, dst_ref, dst_row, sem):
    return pltpu.make_async_copy(src_ref.at[pl.ds(src_row, 1)], dst_ref.at[pl.ds(dst_row, 1)], sem)


def _dispatch_kernel(dest_ref, h_ref, xs_in_ref, xs_ref, sem):
    del xs_in_ref
    tm = h_ref.shape[0]

    def issue(r, _):
        for kk in range(TOP_K):
            _row_copy(h_ref, r, xs_ref, dest_ref[0, 0, TOP_K * r + kk], sem).start()
        return 0

    lax.fori_loop(0, tm, issue, 0, unroll=ROW_DMA_UNROLL)
    for kk in range(TOP_K):
        pltpu.make_async_copy(h_ref, xs_ref.at[pl.ds(0, tm)], sem).wait()


def _dispatch(h, dest, n_slots):
    T, D = h.shape
    tm = min(TILES["disp_tm"], T)
    nt = T // tm
    xs0 = jnp.zeros((n_slots, D), F32)
    return pl.pallas_call(
        _dispatch_kernel,
        out_shape=jax.ShapeDtypeStruct((n_slots, D), F32),
        grid=(nt,),
        in_specs=[pl.BlockSpec((1, 1, TOP_K * tm), lambda i: (i, 0, 0), memory_space=pltpu.SMEM),
                  pl.BlockSpec((tm, D), lambda i: (i, 0)),
                  pl.BlockSpec(memory_space=pl.ANY)],
        out_specs=pl.BlockSpec(memory_space=pl.ANY),
        scratch_shapes=[pltpu.SemaphoreType.DMA],
        input_output_aliases={2: 0},
        compiler_params=_cparams(("arbitrary",)),
        name="moe_dispatch",
    )(dest.reshape(nt, 1, TOP_K * tm), h, xs0)


def _expert_kernel(be_ref, nu_ref, xs_ref, wg_ref, wu_ref, wd_ref, o_ref, xb_ref):
    i = pl.program_id(0)
    j = pl.program_id(1)
    active = i < nu_ref[0]

    @pl.when(j == 0)
    def _():
        o_ref[...] = jnp.zeros_like(o_ref)

    @pl.when(active & (j == 0))
    def _():
        xb_ref[...] = xs_ref[...].astype(BF16)

    @pl.when(active)
    def _():
        xb = xb_ref[...]
        g = jnp.dot(xb, wg_ref[0], preferred_element_type=F32)
        u = jnp.dot(xb, wu_ref[0], preferred_element_type=F32)
        a = (g * _sigmoid(g) * u).astype(BF16)
        _accumulate_dot(o_ref, a, wd_ref)


def _experts(xs, block_e, n_used, wg, wu, wd):
    n_slots, D = xs.shape
    E, _, F = wg.shape
    tm = TILES["exp_tm"]
    tf = min(TILES["exp_tf"], F)
    nb = n_slots // tm
    nj = F // tf

    def clamp(i, nu):
        return jnp.minimum(i, nu[0] - 1)

    def x_map(i, j, be, nu):
        return (clamp(i, nu), 0)

    def w_col_map(i, j, be, nu):
        return (be[clamp(i, nu)], 0, jnp.where(i < nu[0], j, nj - 1))

    def w_row_map(i, j, be, nu):
        return (be[clamp(i, nu)], jnp.where(i < nu[0], j, nj - 1), 0)

    return pl.pallas_call(
        _expert_kernel,
        out_shape=jax.ShapeDtypeStruct((n_slots, D), F32),
        grid_spec=pltpu.PrefetchScalarGridSpec(
            num_scalar_prefetch=2,
            grid=(nb, nj),
            in_specs=[pl.BlockSpec((tm, D), x_map),
                      pl.BlockSpec((1, D, tf), w_col_map),
                      pl.BlockSpec((1, D, tf), w_col_map),
                      pl.BlockSpec((1, tf, D), w_row_map)],
            out_specs=pl.BlockSpec((tm, D), lambda i, j, be, nu: (i, 0)),
            scratch_shapes=[pltpu.VMEM((tm, D), BF16)]),
        compiler_params=_cparams(("arbitrary", "arbitrary")),
        name="moe_experts",
    )(block_e, n_used, xs, wg, wu, wd)


def _combine_kernel(dest_ref, x_ref, wts_ref, gain_ref, gate_ref, ys_ref, o_ref, buf_ref, sem):
    tm = x_ref.shape[0]

    def issue(r, _):
        for kk in range(TOP_K):
            _row_copy(ys_ref, dest_ref[0, 0, TOP_K * r + kk], buf_ref.at[kk], r, sem).start()
        return 0

    lax.fori_loop(0, tm, issue, 0, unroll=ROW_DMA_UNROLL)
    for kk in range(TOP_K):
        pltpu.make_async_copy(ys_ref.at[pl.ds(0, tm)], buf_ref.at[kk], sem).wait()
    w = wts_ref[...]
    y = w[:, 0:1] * buf_ref[0] + w[:, 1:2] * buf_ref[1]
    o_ref[...] = x_ref[...] + gate_ref[0] * _rms(y, gain_ref[...])


def _combine(x2, ys, dest, wts, gain, gate, S):
    T, D = x2.shape
    tm = min(TILES["disp_tm"], S)
    spb = S // tm
    nt = T // tm
    return pl.pallas_call(
        _combine_kernel,
        out_shape=jax.ShapeDtypeStruct((T, D), F32),
        grid=(nt,),
        in_specs=[pl.BlockSpec((1, 1, TOP_K * tm), lambda i: (i, 0, 0), memory_space=pltpu.SMEM),
                  pl.BlockSpec((tm, D), lambda i: (i, 0)),
                  pl.BlockSpec((tm, LANES), lambda i: (i, 0)),
                  pl.BlockSpec((1, D), lambda i: (0, 0)),
                  pl.BlockSpec((1, 1, D), lambda i: (i // spb, 0, 0)),
                  pl.BlockSpec(memory_space=pl.ANY)],
        out_specs=pl.BlockSpec((tm, D), lambda i: (i, 0)),
        scratch_shapes=[pltpu.VMEM((TOP_K, tm, D), F32), pltpu.SemaphoreType.DMA],
        compiler_params=_cparams(("arbitrary",)),
        name="moe_combine",
    )(dest.reshape(nt, 1, TOP_K * tm), x2, wts, gain.reshape(1, D), gate, ys)


def _moe(x2, gain_pre, sc, sh, router_w, router_b, wg, wu, wd, gain_post, gate, S):
    T, D = x2.shape
    E = router_w.shape[1]
    tm_e = TILES["exp_tm"]
    rw = jnp.zeros((D, LANES), F32).at[:, :E].set(router_w)
    rb = jnp.full((1, LANES), NEG_BIG, F32).at[0, :E].set(router_b)
    h, meta, wts, cnt = _router(x2, gain_pre, sc, sh, rw, rb, S)
    counts = cnt[0, :E].astype(jnp.int32)
    padded = ((counts + tm_e - 1) // tm_e) * tm_e
    pend = jnp.cumsum(padded)
    pstart = pend - padded
    dest = pstart[meta[:, 0:TOP_K]] + meta[:, TOP_K:2 * TOP_K]
    n_slots = T * TOP_K + E * tm_e
    nb = n_slots // tm_e
    starts = jnp.arange(nb, dtype=jnp.int32) * tm_e
    block_e = jnp.minimum(jnp.sum((starts[:, None] >= pend[None, :]).astype(jnp.int32), axis=1), E - 1)
    n_used = (pend[-1:] // tm_e).astype(jnp.int32)
    xs = _dispatch(h, dest.reshape(-1), n_slots)
    ys = _experts(xs, block_e, n_used, wg, wu, wd)
    return _combine(x2, ys, dest.reshape(-1), wts, gain_post, gate, S)


def _prep_in_proj(w_in, b_in):
    q_scale = np.float32((MLA_NOPE + MLA_ROPE) ** -0.5 * np.log2(np.e))
    k_scale = np.float32(ML_QK ** -0.5)

    def build(a):
        lead = a.shape[:-1]

        def seg(name):
            s = IN_START[name]
            return a[..., s:s + IN_SIZE[name]]

        mq = seg("mq").reshape(lead + (MLA_HEADS, MLA_NOPE + MLA_ROPE)) * q_scale
        parts = {
            "qn": mq[..., :MLA_NOPE].reshape(lead + (MLA_HEADS * MLA_NOPE,)),
            "qr": mq[..., MLA_NOPE:].reshape(lead + (MLA_HEADS * MLA_ROPE,)),
            "kr": jnp.concatenate([seg("mkr"), jnp.zeros(lead + (LANES - MLA_ROPE,), a.dtype)], axis=-1),
            "gates": jnp.concatenate([seg("li"), seg("lf"),
                                      jnp.zeros(lead + (LANES - 2 * ML_HEADS,), a.dtype)], axis=-1),
            "lk": seg("lk") * k_scale,
            "ckv": seg("mckv"),
            "pad": jnp.zeros(lead + (Z_W["pad"],), a.dtype),
        }
        cols = [parts[n] if n in parts else seg(n) for n, _ in Z_ORDER]
        return jnp.concatenate(cols, axis=-1)

    return build(w_in).astype(BF16), build(b_in)


def kernel(x, c, positions, ada_w, ada_b, norm_pre_mix, norm_post_mix, norm_pre_ffn, norm_post_ffn, w_in, b_in,
           conv_w, sg_norm, sg_w, sg_b, mla_kv_norm, mla_w_uk, mla_w_uv, ml_norm, w_conv_out, w_sg_out,
           w_mla_out, w_ml_out, w_mix_out, ffn_w_gate, ffn_w_up, ffn_w_down, router_w, router_b,
           exp_w_gate, exp_w_up, exp_w_down):
    B, S, D = x.shape
    T = B * S
    x2 = x.reshape(T, D)
    pos = positions.reshape(T, 1).astype(jnp.int32)
    half = MLA_ROPE // 2
    freq32 = ROPE_THETA ** (-jnp.arange(half, dtype=F32) / half)
    freq = jnp.tile(freq32, LANES // half).reshape(1, LANES)
    mod = _ada_mod(c, ada_w, ada_b)
    for l in range(DEPTH):
        sh1, sc1, g1, sh2, sc2, g2 = [m.reshape(B, 1, D) for m in jnp.split(mod[l], 6, axis=-1)]
        w_p, b_p = _prep_in_proj(w_in[l], b_in[l])
        z, zg = _in_proj(x2, norm_pre_mix[l], sc1, sh1, w_p, b_p, S)
        q, k, v = _mla_prep(z, pos, freq, mla_kv_norm[l],
                            mla_w_uk[l].reshape(MLA_KV_RANK, -1).T.astype(BF16),
                            mla_w_uv[l].reshape(MLA_KV_RANK, -1).astype(BF16), B, S)
        y_mla = _attention(q, k, v).reshape(T, MLA_HEADS * MLA_V)
        y_ml = _mlstm(z, zg, ml_norm[l], B, S)
        merged = _merge(z, y_mla, y_ml, conv_w[l], sg_norm[l], sg_w[l], sg_b[l].T,
                        w_conv_out[l].astype(BF16), w_sg_out[l].astype(BF16),
                        w_mla_out[l].astype(BF16), w_ml_out[l].astype(BF16), S)
        x2 = _mix_out(x2, merged, w_mix_out[l].astype(BF16), norm_post_mix[l], g1, S)
        if l % 2 == 0:
            x2 = _ffn_dense(x2, norm_pre_ffn[l], sc2, sh2, ffn_w_gate[l // 2].astype(BF16),
                            ffn_w_up[l // 2].astype(BF16), ffn_w_down[l // 2].astype(BF16),
                            norm_post_ffn[l], g2, S)
        else:
            x2 = _moe(x2, norm_pre_ffn[l], sc2, sh2, router_w[l // 2], router_b[l // 2],
                      exp_w_gate[l // 2].astype(BF16), exp_w_up[l // 2].astype(BF16),
                      exp_w_down[l // 2].astype(BF16), norm_post_ffn[l], g2, S)
    return x2.reshape(B, S, D)
```
